```python
import math
import jax, jax.numpy as jnp
from jax import lax
import numpy as np

D_MODEL = 1024
BATCH = 16
SEQ = 2048
DEPTH = 4

N_MIXERS = 3
D_INNER = D_MODEL
HEAD_DIM = 128
HGRN_HEADS = D_INNER // HEAD_DIM
HGRN_CHUNK = 64
MOBA_HEADS = D_INNER // HEAD_DIM
MOBA_BLOCK = 256
MOBA_TOPK = 3
MOBA_QCHUNK = 32
ROPE_THETA = 10000.0
S5_GROUP = 16
S5_GROUPS = D_INNER // S5_GROUP
S5_STATE = 64
DEEPNORM_ALPHA = (2 * DEPTH) ** 0.25
DEEPNORM_BETA = (8 * DEPTH) ** -0.25
LN_EPS = 1e-5
RMS_EPS = 1e-6
NEG = -1e30

kernel_name = "hybrid_hgrn2_moba_s5_deepnorm"


def layer_norm(x, g, b):
    xf = x.astype(jnp.float32)
    mu = jnp.mean(xf, axis=-1, keepdims=True)
    var = jnp.mean(jnp.square(xf - mu), axis=-1, keepdims=True)
    y = (xf - mu) * lax.rsqrt(var + LN_EPS) * g.astype(jnp.float32) + b.astype(jnp.float32)
    return y.astype(x.dtype)


def hgrn2_lower_bounds(lb_logits):
    sm = jax.nn.softmax(lb_logits.astype(jnp.float32), axis=0)
    return jnp.cumsum(sm, axis=0) - sm[0:1]


def hgrn2_chunk_scan(q, k, v, logf):
    bsz, s, h, dk = q.shape
    dv = v.shape[-1]
    n = s // HGRN_CHUNK

    def to_chunks(t):
        return t.reshape(bsz, n, HGRN_CHUNK, h, t.shape[-1]).transpose(1, 0, 3, 2, 4)

    causal = jnp.tril(jnp.ones((HGRN_CHUNK, HGRN_CHUNK), dtype=bool))

    def step(state, inp):
        qc, kc, vc, gc = inp
        bcum = jnp.cumsum(gc, axis=2)
        diff = bcum[:, :, :, None, :] - bcum[:, :, None, :, :]
        decay = jnp.exp(jnp.where(causal[:, :, None], diff, -jnp.inf))
        scores = jnp.einsum('bhtk,bhtsk,bhsk->bhts', qc, decay, kc)
        o = jnp.einsum('bhts,bhsv->bhtv', scores, vc) + \
            jnp.einsum('bhtk,bhkv->bhtv', qc * jnp.exp(bcum), state)
        b_last = bcum[:, :, -1:, :]
        state = jnp.exp(b_last[:, :, 0, :])[..., None] * state + \
            jnp.einsum('bhsk,bhsv->bhkv', kc * jnp.exp(b_last - bcum), vc)
        return state, o

    state0 = jnp.zeros((bsz, h, dk, dv), jnp.float32)
    _, o = lax.scan(step, state0, (to_chunks(q), to_chunks(k), to_chunks(v), to_chunks(logf)))
    return o.transpose(1, 0, 3, 2, 4).reshape(bsz, s, h, dv)


def hgrn2_mixer(x, w_in, norm_g, w_out, lb):
    bsz, s, _ = x.shape
    q, zf, v, gate = jnp.split(x @ w_in, 4, axis=-1)
    shp = (bsz, s, HGRN_HEADS, HEAD_DIM)
    q = jax.nn.silu(q.astype(jnp.float32)).reshape(shp)
    v = v.astype(jnp.float32).reshape(shp)
    lbh = lb.reshape(HGRN_HEADS, HEAD_DIM)
    f = lbh + (1.0 - lbh) * jax.nn.sigmoid(zf.astype(jnp.float32).reshape(shp))
    logf = jnp.log(f)
    k = -jnp.expm1(logf)
    o = hgrn2_chunk_scan(q, k, v, logf)
    o = o * lax.rsqrt(jnp.mean(jnp.square(o), -1, keepdims=True) + RMS_EPS) * \
        norm_g.astype(jnp.float32).reshape(HGRN_HEADS, HEAD_DIM)
    o = o.reshape(bsz, s, D_INNER) * jax.nn.silu(gate.astype(jnp.float32))
    return o.astype(x.dtype) @ w_out


def rope(t, cos, sin):
    half = t.shape[-1] // 2
    t1, t2 = t[..., :half], t[..., half:]
    return jnp.concatenate([t1 * cos - t2 * sin, t2 * cos + t1 * sin], axis=-1)


def moba_attention(q, k, v):
    bsz, h, s, d = q.shape
    nb = -(-s // MOBA_BLOCK)
    s_pad = nb * MOBA_BLOCK
    pad = ((0, 0), (0, 0), (0, s_pad - s), (0, 0))
    qb = jnp.pad(q, pad).reshape(bsz, h, nb, MOBA_BLOCK, d)
    kb = jnp.pad(k, pad).reshape(bsz, h, nb, MOBA_BLOCK, d)
    vb = jnp.pad(v, pad).reshape(bsz, h, nb, MOBA_BLOCK, d)

    k_mean = jnp.mean(kb, axis=3)
    gate = jnp.einsum('bhtd,bhnd->bhtn', q, k_mean)
    q_blk = jnp.arange(s) // MOBA_BLOCK
    past = jnp.arange(nb)[None, :] < q_blk[:, None]
    gate = jnp.where(past, gate, -jnp.inf)
    topk = min(MOBA_TOPK, nb)
    top_val, top_idx = lax.top_k(gate, topk)
    sel_valid = jnp.isfinite(top_val)

    nqc = s // MOBA_QCHUNK

    def to_qchunks(t):
        return t.reshape(bsz, h, nqc, MOBA_QCHUNK, t.shape[-1]).transpose(0, 2, 1, 3, 4) \
            .reshape(bsz * nqc, h, MOBA_QCHUNK, t.shape[-1])

    b_ids = jnp.repeat(jnp.arange(bsz), nqc)
    h_ids = jnp.arange(h)[:, None, None]

    def past_chunk(args):
        qc, ic, mc, bi = args
        kg = kb[bi][h_ids, ic]
        vg = vb[bi][h_ids, ic]
        sc = jnp.einsum('hqd,hqjsd->hqjs', qc, kg)
        sc = jnp.where(mc[..., None], sc, NEG)
        m = jnp.max(sc, axis=(2, 3))
        p = jnp.exp(sc - m[..., None, None]) * mc[..., None]
        return m, jnp.sum(p, axis=(2, 3)), jnp.einsum('hqjs,hqjsd->hqd', p, vg)

    m_p, l_p, acc_p = lax.map(past_chunk, (to_qchunks(q), to_qchunks(top_idx),
                                           to_qchunks(sel_valid), b_ids))
    m_p = m_p.reshape(bsz, nqc, h, MOBA_QCHUNK).transpose(0, 2, 1, 3).reshape(bsz, h, s)
    l_p = l_p.reshape(bsz, nqc, h, MOBA_QCHUNK).transpose(0, 2, 1, 3).reshape(bsz, h, s)
    acc_p = acc_p.reshape(bsz, nqc, h, MOBA_QCHUNK, d).transpose(0, 2, 1, 3, 4).reshape(bsz, h, s, d)

    causal = jnp.tril(jnp.ones((MOBA_BLOCK, MOBA_BLOCK), dtype=bool))
    s_o = jnp.where(causal, jnp.einsum('bhnqd,bhnsd->bhnqs', qb, kb), NEG)
    m_o = jnp.max(s_o, axis=-1)
    p_o = jnp.exp(s_o - m_o[..., None])
    l_o = jnp.sum(p_o, axis=-1).reshape(bsz, h, s_pad)[:, :, :s]
    acc_o = jnp.einsum('bhnqs,bhnsd->bhnqd', p_o, vb).reshape(bsz, h, s_pad, d)[:, :, :s]
    m_o = m_o.reshape(bsz, h, s_pad)[:, :, :s]

    m = jnp.maximum(m_p, m_o)
    a_p = jnp.exp(m_p - m)
    a_o = jnp.exp(m_o - m)
    return (acc_p * a_p[..., None] + acc_o * a_o[..., None]) / (l_p * a_p + l_o * a_o)[..., None]


def moba_mixer(x, w_in, w_out):
    bsz, s, _ = x.shape
    q, k, v, gate = jnp.split(x @ w_in, 4, axis=-1)
    shp = (bsz, s, MOBA_HEADS, HEAD_DIM)
    q = q.astype(jnp.float32).reshape(shp)
    k = k.astype(jnp.float32).reshape(shp)
    v = v.astype(jnp.float32).reshape(shp)
    pos = jnp.arange(s, dtype=jnp.float32)
    inv_freq = 1.0 / (ROPE_THETA ** (jnp.arange(0, HEAD_DIM, 2, dtype=jnp.float32) / HEAD_DIM))
    ang = pos[:, None] * inv_freq[None, :]
    cos, sin = jnp.cos(ang)[:, None, :], jnp.sin(ang)[:, None, :]
    q = rope(q, cos, sin) * (HEAD_DIM ** -0.5)
    k = rope(k, cos, sin)
    o = moba_attention(q.transpose(0, 2, 1, 3), k.transpose(0, 2, 1, 3), v.transpose(0, 2, 1, 3))
    o = o.transpose(0, 2, 1, 3).reshape(bsz, s, D_INNER) * jax.nn.silu(gate.astype(jnp.float32))
    return o.astype(x.dtype) @ w_out


def s5_mixer(x, w_in, a_re, a_im, log_dt, b_re, b_im, c_re, c_im, d_skip, w_glu, b_glu, w_out):
    bsz, s, _ = x.shape
    u, gate = jnp.split(x @ w_in, 2, axis=-1)
    u32 = u.astype(jnp.float32)
    ar, ai = a_re.astype(jnp.float32), a_im.astype(jnp.float32)
    dt = jnp.exp(log_dt.astype(jnp.float32))[:, None]
    mag = jnp.exp(dt * ar)
    abar_re, abar_im = mag * jnp.cos(dt * ai), mag * jnp.sin(dt * ai)
    nr, ni = abar_re - 1.0, abar_im
    den = ar * ar + ai * ai
    z_re = (nr * ar + ni * ai) / den
    z_im = (ni * ar - nr * ai) / den
    br, bi = b_re.astype(jnp.float32), b_im.astype(jnp.float32)
    bbar_re = z_re[..., None] * br - z_im[..., None] * bi
    bbar_im = z_re[..., None] * bi + z_im[..., None] * br
    cr, ci = c_re.astype(jnp.float32), c_im.astype(jnp.float32)

    def combine(e1, e2):
        a1r, a1i, b1r, b1i = e1
        a2r, a2i, b2r, b2i = e2
        return (a2r * a1r - a2i * a1i, a2r * a1i + a2i * a1r,
                a2r * b1r - a2i * b1i + b2r, a2r * b1i + a2i * b1r + b2i)

    def scan_one(u_b):
        bu_re = jnp.einsum('gph,sgh->sgp', bbar_re, u_b)
        bu_im = jnp.einsum('gph,sgh->sgp', bbar_im, u_b)
        ar_s = jnp.broadcast_to(abar_re, bu_re.shape)
        ai_s = jnp.broadcast_to(abar_im, bu_re.shape)
        _, _, hr, hi = lax.associative_scan(combine, (ar_s, ai_s, bu_re, bu_im), axis=0)
        return jnp.einsum('ghp,sgp->sgh', cr, hr) - jnp.einsum('ghp,sgp->sgh', ci, hi)

    y = lax.map(scan_one, u32.reshape(bsz, s, S5_GROUPS, S5_GROUP)).reshape(bsz, s, D_INNER)
    y = y + d_skip.astype(jnp.float32) * u32
    y = jax.nn.gelu(y)
    y = y * jax.nn.sigmoid(y @ w_glu.astype(jnp.float32) + b_glu.astype(jnp.float32))
    y = y * jax.nn.silu(gate.astype(jnp.float32))
    return y.astype(x.dtype) @ w_out


def setup_inputs(seed: int = 0) -> dict:
    key = jax.random.key(seed)
    keys = iter(jax.random.split(key, 64))
    f32 = jnp.float32

    def nrm(shape, scale):
        return jax.random.normal(next(keys), shape, f32) * scale

    inp = {}
    inp["x"] = nrm((BATCH, SEQ, D_MODEL), 1.0)
    inp["hgrn_lower_bounds"] = nrm((DEPTH, D_INNER), 0.1)
    w_out_scale = DEEPNORM_BETA * D_INNER ** -0.5
    for i in range(DEPTH):
        kind = i % N_MIXERS
        p = f"l{i}_"
        if kind == 0:
            inp[p + "w_in"] = nrm((D_MODEL, 4 * D_INNER), D_MODEL ** -0.5)
            inp[p + "norm_g"] = 1.0 + nrm((D_INNER,), 0.02)
            inp[p + "w_out"] = nrm((D_INNER, D_MODEL), w_out_scale)
        elif kind == 1:
            inp[p + "w_in"] = nrm((D_MODEL, 4 * D_INNER), D_MODEL ** -0.5)
            inp[p + "w_out"] = nrm((D_INNER, D_MODEL), w_out_scale)
        else:
            inp[p + "w_in"] = nrm((D_MODEL, 2 * D_INNER), D_MODEL ** -0.5)
            inp[p + "a_re"] = -0.5 + nrm((S5_GROUPS, S5_STATE), 0.01)
            inp[p + "a_im"] = math.pi * jnp.broadcast_to(jnp.arange(S5_STATE, dtype=f32), (S5_GROUPS, S5_STATE)) \
                + nrm((S5_GROUPS, S5_STATE), 0.01)
            inp[p + "log_dt"] = jax.random.uniform(next(keys), (S5_GROUPS,), f32,
                                                   minval=math.log(1e-3), maxval=math.log(1e-1))
            inp[p + "b_re"] = nrm((S5_GROUPS, S5_STATE, S5_GROUP), (2 * S5_GROUP) ** -0.5)
            inp[p + "b_im"] = nrm((S5_GROUPS, S5_STATE, S5_GROUP), (2 * S5_GROUP) ** -0.5)
            inp[p + "c_re"] = nrm((S5_GROUPS, S5_GROUP, S5_STATE), (2 * S5_STATE) ** -0.5)
            inp[p + "c_im"] = nrm((S5_GROUPS, S5_GROUP, S5_STATE), (2 * S5_STATE) ** -0.5)
            inp[p + "d"] = nrm((D_INNER,), 1.0)
            inp[p + "w_glu"] = nrm((D_INNER, D_INNER), D_INNER ** -0.5)
            inp[p + "b_glu"] = nrm((D_INNER,), 0.01)
            inp[p + "w_out"] = nrm((D_INNER, D_MODEL), w_out_scale)
        inp[p + "ln_g"] = 1.0 + nrm((D_MODEL,), 0.02)
        inp[p + "ln_b"] = nrm((D_MODEL,), 0.01)
    return inp


def reference(x, hgrn_lower_bounds,
              l0_w_in, l0_norm_g, l0_w_out, l0_ln_g, l0_ln_b,
              l1_w_in, l1_w_out, l1_ln_g, l1_ln_b,
              l2_w_in, l2_a_re, l2_a_im, l2_log_dt, l2_b_re, l2_b_im, l2_c_re, l2_c_im,
              l2_d, l2_w_glu, l2_b_glu, l2_w_out, l2_ln_g, l2_ln_b,
              l3_w_in, l3_norm_g, l3_w_out, l3_ln_g, l3_ln_b):
    lbs = hgrn2_lower_bounds(hgrn_lower_bounds)
    layers = [
        dict(w_in=l0_w_in, norm_g=l0_norm_g, w_out=l0_w_out, ln_g=l0_ln_g, ln_b=l0_ln_b),
        dict(w_in=l1_w_in, w_out=l1_w_out, ln_g=l1_ln_g, ln_b=l1_ln_b),
        dict(w_in=l2_w_in, a_re=l2_a_re, a_im=l2_a_im, log_dt=l2_log_dt, b_re=l2_b_re, b_im=l2_b_im,
             c_re=l2_c_re, c_im=l2_c_im, d=l2_d, w_glu=l2_w_glu, b_glu=l2_b_glu, w_out=l2_w_out,
             ln_g=l2_ln_g, ln_b=l2_ln_b),
        dict(w_in=l3_w_in, norm_g=l3_norm_g, w_out=l3_w_out, ln_g=l3_ln_g, ln_b=l3_ln_b),
    ]
    for i in range(DEPTH):
        p = layers[i]
        kind = i % N_MIXERS
        if kind == 0:
            y = hgrn2_mixer(x, p["w_in"], p["norm_g"], p["w_out"], lbs[i])
        elif kind == 1:
            y = moba_mixer(x, p["w_in"], p["w_out"])
        else:
            y = s5_mixer(x, p["w_in"], p["a_re"], p["a_im"], p["log_dt"], p["b_re"], p["b_im"],
                         p["c_re"], p["c_im"], p["d"], p["w_glu"], p["b_glu"], p["w_out"])
        x = layer_norm(DEEPNORM_ALPHA * x + y, p["ln_g"], p["ln_b"])
    return x
```

```python
import functools
import math

import numpy as np
import jax
import jax.numpy as jnp
from jax import lax
from jax.experimental import pallas as pl
from jax.experimental.pallas import tpu as pltpu

D_MODEL = 1024
BATCH = 16
SEQ = 2048
DEPTH = 4
HEADS = 8
HEAD_DIM = 128
MOBA_BLOCK = 256
MOBA_TOPK = 3
ROPE_THETA = 10000.0
S5_GROUP = 16
S5_GROUPS = 64
S5_STATE = 64
ALPHA = (2 * DEPTH) ** 0.25
LN_EPS = 1e-5
RMS_EPS = 1e-6
NEG = -1e30

VMEM_LIMIT_BYTES = 56 * 1024 * 1024

HGRN_TILE = 256
HGRN_CHUNK = 128
PROJ_TILE = 512
S5_STEPS = 32
S5_SLAB = 256
S5_NSLAB = D_MODEL // S5_SLAB
S5_SLAB_STATES = (S5_SLAB // S5_GROUP) * S5_STATE

F32 = jnp.float32
BF16 = jnp.bfloat16


def _dot(a, b):
    return jnp.dot(a, b, preferred_element_type=F32)


def _dot_nt(a, b):
    return lax.dot_general(a, b, (((1,), (1,)), ((), ())), preferred_element_type=F32)


def _split3(a):
    hi = a.astype(BF16)
    r1 = a - hi.astype(F32)
    mid = r1.astype(BF16)
    lo = (r1 - mid.astype(F32)).astype(BF16)
    return hi, mid, lo


def _residual_layer_norm(x, y, g, b):
    r = ALPHA * x + y
    mu = jnp.mean(r, axis=-1, keepdims=True)
    c = r - mu
    var = jnp.mean(c * c, axis=-1, keepdims=True)
    return c * lax.rsqrt(var + LN_EPS) * g + b


def _const_spec(shape):
    nd = len(shape)
    return pl.BlockSpec(shape, lambda *_: (0,) * nd, pipeline_mode=pl.Buffered(1))


def _hgrn_level_table(c):
    t = np.arange(c)[:, None]
    s = np.arange(c)[None, :]
    x = np.bitwise_xor(t, s)
    lv = np.zeros((c, c), np.int32)
    nz = x > 0
    lv[nz] = 2 ** (np.floor(np.log2(x[nz])).astype(np.int32) + 1)
    lv[s >= t] = 0
    return lv


def _segment_mid_rows(bc, seg):
    c, w = bc.shape
    half = seg // 2
    if seg >= 8:
        pieces = [jnp.broadcast_to(bc[a + half - 1:a + half, :], (seg, w)) for a in range(0, c, seg)]
        return jnp.concatenate(pieces, axis=0)
    row = lax.broadcasted_iota(jnp.int32, (8, w), 0)
    pieces = []
    for a in range(0, c, 8):
        cand = [jnp.broadcast_to(bc[a + o + half - 1:a + o + half, :], (8, w)) for o in range(0, 8, seg)]
        out = cand[-1]
        for idx in range(len(cand) - 2, -1, -1):
            out = jnp.where(row < (idx + 1) * seg, cand[idx], out)
        pieces.append(out)
    return jnp.concatenate(pieces, axis=0)


def _hgrn_kernel(layer, x_ref, lbl_ref, lv_ref, tri_ref, w_in_ref, ng_ref, w_out_ref, lng_ref, lnb_ref,
                 out_ref, st_ref, o_ref):
    c = HGRN_CHUNK

    @pl.when(pl.program_id(1) == 0)
    def _():
        st_ref[...] = jnp.zeros_like(st_ref)

    x = x_ref[...]
    z = _dot(x.astype(BF16), w_in_ref[...])

    lbl = lbl_ref[...]
    e = jnp.exp(lbl - jnp.max(lbl, axis=0, keepdims=True))
    sm = e / jnp.sum(e, axis=0, keepdims=True)
    lb = jnp.zeros((1, D_MODEL), F32)
    for r in range(1, layer + 1):
        lb = lb + sm[r:r + 1, :]

    q_all = jax.nn.silu(z[:, 0:D_MODEL])
    f_all = lb + (1.0 - lb) * jax.nn.sigmoid(z[:, D_MODEL:2 * D_MODEL])
    g_all = jnp.log(f_all)
    k_all = 1.0 - f_all
    v_all = z[:, 2 * D_MODEL:3 * D_MODEL]
    gate_all = jax.nn.silu(z[:, 3 * D_MODEL:4 * D_MODEL])

    lv = lv_ref[...]
    tri = tri_ref[...]
    ng = ng_ref[...]
    row_in_chunk = lax.broadcasted_iota(jnp.int32, (c, D_MODEL), 0)

    for ci in range(HGRN_TILE // c):
        r0 = ci * c
        q = q_all[r0:r0 + c]
        k = k_all[r0:r0 + c]
        v = v_all[r0:r0 + c]
        g = g_all[r0:r0 + c]
        g_hi, g_mid, g_lo = _split3(g)
        bc = _dot(tri, g_hi) + _dot(tri, g_mid) + _dot(tri, g_lo)

        w_levels = []
        seg = c
        while seg >= 2:
            mid = _segment_mid_rows(bc, seg)
            decay = jnp.exp(-jnp.abs(bc - mid))
            second = (row_in_chunk & (seg - 1)) >= (seg // 2)
            w_levels.append((seg, (jnp.where(second, q, k) * decay).astype(BF16)))
            seg //= 2

        qd = (q * jnp.exp(bc)).astype(BF16)
        b_last = bc[c - 1:c, :]
        kd = (k * jnp.exp(b_last - bc)).astype(BF16)
        d_last = jnp.exp(b_last)
        qk = q * k
        v_bf = v.astype(BF16)

        for h in range(HEADS):
            sl = slice(h * HEAD_DIM, (h + 1) * HEAD_DIM)
            scores = jnp.zeros((c, c), F32)
            for seg, w in w_levels:
                wh = w[:, sl]
                scores = jnp.where(lv == seg, _dot_nt(wh, wh), scores)
            st = st_ref[h]
            o = _dot(scores.astype(BF16), v_bf[:, sl])
            o = o + jnp.sum(qk[:, sl], axis=-1, keepdims=True) * v[:, sl]
            o = o + _dot_nt(qd[:, sl], st.astype(BF16))
            st_ref[h] = st * d_last[:, sl] + _dot(v[:, sl].T.astype(BF16), kd[:, sl])
            o = o * lax.rsqrt(jnp.mean(o * o, axis=-1, keepdims=True) + RMS_EPS) * ng[:, sl]
            o_ref[r0:r0 + c, sl] = (o * gate_all[r0:r0 + c, sl]).astype(BF16)

    y = _dot(o_ref[...], w_out_ref[...])
    out_ref[...] = _residual_layer_norm(x, y, lng_ref[...], lnb_ref[...])


def _hgrn_layer(x2d, lb_logits, w_in, norm_g, w_out, ln_g, ln_b, *, layer, in_tb, out_tb):
    nt = SEQ // HGRN_TILE
    tile = (HGRN_TILE, D_MODEL)
    tb_spec = pl.BlockSpec(tile, lambda b, c: (c, b))
    bt_spec = pl.BlockSpec(tile, lambda b, c: (b * nt + c, 0))
    out_shape = (SEQ, BATCH * D_MODEL) if out_tb else (BATCH * SEQ, D_MODEL)
    lv = jnp.asarray(_hgrn_level_table(HGRN_CHUNK))
    tri = jnp.asarray(np.tril(np.ones((HGRN_CHUNK, HGRN_CHUNK), np.float32)), dtype=BF16)
    return pl.pallas_call(
        functools.partial(_hgrn_kernel, layer),
        grid=(BATCH, nt),
        in_specs=[
            tb_spec if in_tb else bt_spec,
            _const_spec((DEPTH, D_MODEL)),
            _const_spec((HGRN_CHUNK, HGRN_CHUNK)),
            _const_spec((HGRN_CHUNK, HGRN_CHUNK)),
            _const_spec((D_MODEL, 4 * D_MODEL)),
            _const_spec((1, D_MODEL)),
            _const_spec((D_MODEL, D_MODEL)),
            _const_spec((1, D_MODEL)),
            _const_spec((1, D_MODEL)),
        ],
        out_specs=tb_spec if out_tb else bt_spec,
        out_shape=jax.ShapeDtypeStruct(out_shape, F32),
        scratch_shapes=[
            pltpu.VMEM((HEADS, HEAD_DIM, HEAD_DIM), F32),
            pltpu.VMEM((HGRN_TILE, D_MODEL), BF16),
        ],
        compiler_params=pltpu.CompilerParams(
            dimension_semantics=("arbitrary", "arbitrary"),
            vmem_limit_bytes=VMEM_LIMIT_BYTES),
        name=f"hgrn2_layer{layer}",
    )(x2d, lb_logits, lv, tri, w_in.astype(BF16), norm_g.reshape(1, D_MODEL), w_out.astype(BF16),
      ln_g.reshape(1, D_MODEL), ln_b.reshape(1, D_MODEL))


def _moba_proj_kernel(x_ref, w_in_ref, cq_ref, sq_ref, ck_ref, sk_ref, q_ref, k_ref, v_ref, gate_ref):
    z = _dot(x_ref[...].astype(BF16), w_in_ref[...])
    cq, sq, ck, sk = cq_ref[...], sq_ref[...], ck_ref[...], sk_ref[...]
    for h in range(HEADS):
        qh = z[:, h * HEAD_DIM:(h + 1) * HEAD_DIM]
        kh = z[:, D_MODEL + h * HEAD_DIM:D_MODEL + (h + 1) * HEAD_DIM]
        q_ref[0, h] = (qh * cq + pltpu.roll(qh, HEAD_DIM // 2, 1) * sq).astype(BF16)
        k_ref[0, h] = (kh * ck + pltpu.roll(kh, HEAD_DIM // 2, 1) * sk).astype(BF16)
        v_ref[0, h] = z[:, 2 * D_MODEL + h * HEAD_DIM:2 * D_MODEL + (h + 1) * HEAD_DIM].astype(BF16)
    gate_ref[...] = jax.nn.silu(z[:, 3 * D_MODEL:4 * D_MODEL]).astype(BF16)


def _moba_attn_kernel(q_ref, k_ref, v_ref, gate_ref, x_ref, w_out_ref, lng_ref, lnb_ref,
                      out_ref, km_ref, o_ref):
    n = pl.program_id(1)
    blk = MOBA_BLOCK

    @pl.when(n == 0)
    def _():
        for h in range(HEADS):
            kf = k_ref[0, h].astype(F32).reshape(SEQ // blk, blk, HEAD_DIM)
            km_ref[h] = jnp.mean(kf, axis=1)

    nb = SEQ // blk
    lane = lax.broadcasted_iota(jnp.int32, (blk, nb), 1)
    row = lax.broadcasted_iota(jnp.int32, (blk, blk), 0)
    col = lax.broadcasted_iota(jnp.int32, (blk, blk), 1)
    causal = col <= row

    def head_body(h, carry):
        q = q_ref[0, h]
        km = km_ref[h]
        km_hi = km.astype(BF16)
        km_lo = (km - km_hi.astype(F32)).astype(BF16)
        gsc = _dot_nt(q, km_hi) + _dot_nt(q, km_lo)
        past = lane < n
        sel = jnp.zeros((blk, nb), F32)
        for j in range(nb - 1):
            cj = gsc[:, j:j + 1]
            beats = past & ((gsc > cj) | ((gsc == cj) & (lane < j)))
            cnt = jnp.sum(beats.astype(F32), axis=1, keepdims=True)
            sel = jnp.where((lane == j) & (cnt < MOBA_TOPK) & past, 1.0, sel)

        k_own = k_ref[0, h, pl.ds(pl.multiple_of(n * blk, blk), blk), :]
        v_own = v_ref[0, h, pl.ds(pl.multiple_of(n * blk, blk), blk), :]
        s = jnp.where(causal, _dot_nt(q, k_own), NEG)
        m0 = jnp.max(s, axis=1, keepdims=True)
        p = jnp.exp(s - m0)
        l0 = jnp.sum(p, axis=1, keepdims=True)
        acc0 = _dot(p.astype(BF16), v_own)

        def past_body(j, mla):
            m, l, acc = mla
            kj = k_ref[0, h, pl.ds(pl.multiple_of(j * blk, blk), blk), :]
            vj = v_ref[0, h, pl.ds(pl.multiple_of(j * blk, blk), blk), :]
            selj = jnp.sum(jnp.where(lane == j, sel, 0.0), axis=1, keepdims=True) > 0.5
            sj = jnp.where(selj, _dot_nt(q, kj), NEG)
            m_new = jnp.maximum(m, jnp.max(sj, axis=1, keepdims=True))
            a = jnp.exp(m - m_new)
            pj = jnp.exp(sj - m_new)
            l = a * l + jnp.sum(pj, axis=1, keepdims=True)
            acc = a * acc + _dot(pj.astype(BF16), vj)
            return m_new, l, acc

        _, l, acc = lax.fori_loop(0, n, past_body, (m0, l0, acc0))
        o_ref[h] = acc / l
        return carry

    lax.fori_loop(0, HEADS, head_body, 0)

    o = jnp.concatenate([o_ref[h] for h in range(HEADS)], axis=1)
    o = (o * gate_ref[...].astype(F32)).astype(BF16)
    y = _dot(o, w_out_ref[...])
    out_ref[...] = _residual_layer_norm(x_ref[...], y, lng_ref[...], lnb_ref[...])


def _rope_tables():
    pos = np.arange(SEQ, dtype=np.float32)
    inv_freq = (1.0 / (np.float32(ROPE_THETA) ** (np.arange(0, HEAD_DIM, 2, dtype=np.float32) / HEAD_DIM))
                ).astype(np.float32)
    ang = pos[:, None] * inv_freq[None, :]
    cos, sin = np.cos(ang).astype(np.float32), np.sin(ang).astype(np.float32)
    c = np.concatenate([cos, cos], axis=1)
    s = np.concatenate([-sin, sin], axis=1)
    scale = np.float32(HEAD_DIM ** -0.5)
    return c * scale, s * scale, c, s


def _moba_layer(x2d, w_in, w_out, ln_g, ln_b):
    tiles_per_seq = SEQ // PROJ_TILE
    hshape = (BATCH, HEADS, SEQ, HEAD_DIM)
    hspec = pl.BlockSpec((1, HEADS, PROJ_TILE, HEAD_DIM), lambda i: (i // tiles_per_seq, 0, i % tiles_per_seq, 0))
    tspec = pl.BlockSpec((PROJ_TILE, HEAD_DIM), lambda i: (i % tiles_per_seq, 0))
    cq, sq, ck, sk = (jnp.asarray(t) for t in _rope_tables())
    q, k, v, gate = pl.pallas_call(
        _moba_proj_kernel,
        grid=(BATCH * tiles_per_seq,),
        in_specs=[
            pl.BlockSpec((PROJ_TILE, D_MODEL), lambda i: (i, 0)),
            _const_spec((D_MODEL, 4 * D_MODEL)),
            tspec, tspec, tspec, tspec,
        ],
        out_specs=[hspec, hspec, hspec, pl.BlockSpec((PROJ_TILE, D_MODEL), lambda i: (i, 0))],
        out_shape=[jax.ShapeDtypeStruct(hshape, BF16)] * 3 + [jax.ShapeDtypeStruct((BATCH * SEQ, D_MODEL), BF16)],
        compiler_params=pltpu.CompilerParams(
            dimension_semantics=("arbitrary",), vmem_limit_bytes=VMEM_LIMIT_BYTES),
        name="moba_proj",
    )(x2d, w_in.astype(BF16), cq, sq, ck, sk)

    nb = SEQ // MOBA_BLOCK
    kv_spec = pl.BlockSpec((1, HEADS, SEQ, HEAD_DIM), lambda b, n: (b, 0, 0, 0))
    row_spec = pl.BlockSpec((MOBA_BLOCK, D_MODEL), lambda b, n: (b * nb + n, 0))
    return pl.pallas_call(
        _moba_attn_kernel,
        grid=(BATCH, nb),
        in_specs=[
            pl.BlockSpec((1, HEADS, MOBA_BLOCK, HEAD_DIM), lambda b, n: (b, 0, n, 0)),
            kv_spec, kv_spec, row_spec, row_spec,
            _const_spec((D_MODEL, D_MODEL)),
            _const_spec((1, D_MODEL)),
            _const_spec((1, D_MODEL)),
        ],
        out_specs=pl.BlockSpec((MOBA_BLOCK, D_MODEL), lambda b, n: (n, b)),
        out_shape=jax.ShapeDtypeStruct((SEQ, BATCH * D_MODEL), F32),
        scratch_shapes=[
            pltpu.VMEM((HEADS, nb, HEAD_DIM), F32),
            pltpu.VMEM((HEADS, MOBA_BLOCK, HEAD_DIM), F32),
        ],
        compiler_params=pltpu.CompilerParams(
            dimension_semantics=("arbitrary", "arbitrary"), vmem_limit_bytes=VMEM_LIMIT_BYTES),
        name="moba_attn",
    )(q, k, v, gate, x2d, w_out.astype(BF16), ln_g.reshape(1, D_MODEL), ln_b.reshape(1, D_MODEL))


def _s5_discretize_kernel(ar_ref, ai_ref, ldt_ref, br_ref, bi_ref, abr_ref, abi_ref, bbr_ref, bbi_ref):
    ar, ai = ar_ref[...], ai_ref[...]
    dt = jnp.exp(ldt_ref[...])
    mag = jnp.exp(dt * ar)
    abr = mag * jnp.cos(dt * ai)
    abi = mag * jnp.sin(dt * ai)
    nr, ni = abr - 1.0, abi
    den = ar * ar + ai * ai
    zr = (nr * ar + ni * ai) / den
    zi = (ni * ar - nr * ai) / den
    br, bi = br_ref[...], bi_ref[...]
    abr_ref[...] = abr
    abi_ref[...] = abi
    bbr_ref[...] = zr * br - zi * bi
    bbi_ref[...] = zr * bi + zi * br


def _s5_kernel(x_ref, w_in_ref, bd_ref, lam_ref, cd_ref, d_ref, w_glu_ref, b_glu_ref, w_out_ref,
               lng_ref, lnb_ref, out_ref, hst_ref, xs_ref, hb_ref, y_ref):
    ns = S5_SLAB_STATES

    @pl.when(pl.program_id(0) == 0)
    def _():
        hst_ref[...] = jnp.zeros_like(hst_ref)

    x = x_ref[...]
    z = _dot(x.astype(BF16), w_in_ref[...])
    u = z[:, 0:D_MODEL]
    gate = z[:, D_MODEL:2 * D_MODEL]
    u_bf = u.astype(BF16)

    for s in range(S5_NSLAB):
        xs_ref[...] = _dot(u_bf[:, s * S5_SLAB:(s + 1) * S5_SLAB], bd_ref[s])
        lr = jnp.broadcast_to(lam_ref[s, 0:1, :], (BATCH, ns))
        li = jnp.broadcast_to(lam_ref[s, 1:2, :], (BATCH, ns))

        def step(t, h):
            hr, hi = h
            r = pl.multiple_of(t * BATCH, BATCH)
            xr = xs_ref[pl.ds(r, BATCH), 0:ns]
            xi = xs_ref[pl.ds(r, BATCH), ns:2 * ns]
            nhr = lr * hr - li * hi + xr
            nhi = lr * hi + li * hr + xi
            hb_ref[pl.ds(r, BATCH), 0:ns] = nhr.astype(BF16)
            hb_ref[pl.ds(r, BATCH), ns:2 * ns] = nhi.astype(BF16)
            return nhr, nhi

        hr, hi = lax.fori_loop(0, S5_STEPS, step, (hst_ref[s, 0], hst_ref[s, 1]))
        hst_ref[s, 0] = hr
        hst_ref[s, 1] = hi
        y_ref[:, s * S5_SLAB:(s + 1) * S5_SLAB] = _dot(hb_ref[...], cd_ref[s])

    y = y_ref[...] + d_ref[...] * u
    y = jax.nn.gelu(y)
    y = y * jax.nn.sigmoid(_dot(y.astype(BF16), w_glu_ref[...]) + b_glu_ref[...])
    y = y * jax.nn.silu(gate)
    out = _dot(y.astype(BF16), w_out_ref[...])
    out_ref[...] = _residual_layer_norm(x, out, lng_ref[...], lnb_ref[...])


def _s5_layer(x_tb, w_in, a_re, a_im, log_dt, b_re, b_im, c_re, c_im, d_skip, w_glu, b_glu, w_out, ln_g, ln_b):
    gh = S5_GROUPS * S5_GROUP
    rep = lambda a: jnp.repeat(a, S5_GROUP, axis=0)
    small = jax.ShapeDtypeStruct((gh, S5_STATE), F32)
    abr, abi, bbr, bbi = pl.pallas_call(
        _s5_discretize_kernel,
        out_shape=[small] * 4,
        name="s5_discretize",
    )(rep(a_re), rep(a_im), rep(jnp.broadcast_to(log_dt[:, None], (S5_GROUPS, S5_STATE))),
      b_re.transpose(0, 2, 1).reshape(gh, S5_STATE), b_im.transpose(0, 2, 1).reshape(gh, S5_STATE))

    gl = S5_SLAB // S5_GROUP
    eye = jnp.eye(gl, dtype=F32)

    def block_diag_in(bb):
        t = bb.reshape(S5_NSLAB, gl, S5_GROUP, S5_STATE)
        return jnp.einsum('sghp,gk->sghkp', t, eye).reshape(S5_NSLAB, S5_SLAB, gl * S5_STATE)

    def block_diag_out(cc):
        t = cc.reshape(S5_NSLAB, gl, S5_GROUP, S5_STATE)
        return jnp.einsum('sghp,gk->skpgh', t, eye).reshape(S5_NSLAB, gl * S5_STATE, S5_SLAB)

    bd = jnp.concatenate([block_diag_in(bbr), block_diag_in(bbi)], axis=2).astype(BF16)
    cd = jnp.concatenate([block_diag_out(c_re), -block_diag_out(c_im)], axis=1).astype(BF16)
    lam = jnp.stack([abr[::S5_GROUP].reshape(S5_NSLAB, gl * S5_STATE),
                     abi[::S5_GROUP].reshape(S5_NSLAB, gl * S5_STATE)], axis=1)

    rows = S5_STEPS * BATCH
    ns2 = 2 * S5_SLAB_STATES
    row_spec = pl.BlockSpec((rows, D_MODEL), lambda i: (i, 0))
    return pl.pallas_call(
        _s5_kernel,
        grid=(SEQ // S5_STEPS,),
        in_specs=[
            row_spec,
            _const_spec((D_MODEL, 2 * D_MODEL)),
            _const_spec((S5_NSLAB, S5_SLAB, ns2)),
            _const_spec((S5_NSLAB, 2, S5_SLAB_STATES)),
            _const_spec((S5_NSLAB, ns2, S5_SLAB)),
            _const_spec((1, D_MODEL)),
            _const_spec((D_MODEL, D_MODEL)),
            _const_spec((1, D_MODEL)),
            _const_spec((D_MODEL, D_MODEL)),
            _const_spec((1, D_MODEL)),
            _const_spec((1, D_MODEL)),
        ],
        out_specs=row_spec,
        out_shape=jax.ShapeDtypeStruct((SEQ * BATCH, D_MODEL), F32),
        scratch_shapes=[
            pltpu.VMEM((S5_NSLAB, 2, BATCH, S5_SLAB_STATES), F32),
            pltpu.VMEM((rows, ns2), F32),
            pltpu.VMEM((rows, ns2), BF16),
            pltpu.VMEM((rows, D_MODEL), F32),
        ],
        compiler_params=pltpu.CompilerParams(
            dimension_semantics=("arbitrary",), vmem_limit_bytes=VMEM_LIMIT_BYTES),
        name="s5_layer",
    )(x_tb, w_in.astype(BF16), bd, lam, cd, d_skip.reshape(1, D_MODEL), w_glu.astype(BF16),
      b_glu.reshape(1, D_MODEL), w_out.astype(BF16), ln_g.reshape(1, D_MODEL), ln_b.reshape(1, D_MODEL))


def kernel(x, hgrn_lower_bounds,
           l0_w_in, l0_norm_g, l0_w_out, l0_ln_g, l0_ln_b,
           l1_w_in, l1_w_out, l1_ln_g, l1_ln_b,
           l2_w_in, l2_a_re, l2_a_im, l2_log_dt, l2_b_re, l2_b_im, l2_c_re, l2_c_im,
           l2_d, l2_w_glu, l2_b_glu, l2_w_out, l2_ln_g, l2_ln_b,
           l3_w_in, l3_norm_g, l3_w_out, l3_ln_g, l3_ln_b):
    h = x.reshape(BATCH * SEQ, D_MODEL)
    h = _hgrn_layer(h, hgrn_lower_bounds, l0_w_in, l0_norm_g, l0_w_out, l0_ln_g, l0_ln_b,
                    layer=0, in_tb=False, out_tb=False)
    h = _moba_layer(h, l1_w_in, l1_w_out, l1_ln_g, l1_ln_b)
    h = _s5_layer(h.reshape(SEQ * BATCH, D_MODEL), l2_w_in, l2_a_re, l2_a_im, l2_log_dt, l2_b_re, l2_b_im,
                  l2_c_re, l2_c_im, l2_d, l2_w_glu, l2_b_glu, l2_w_out, l2_ln_g, l2_ln_b)
    h = _hgrn_layer(h.reshape(SEQ, BATCH * D_MODEL), hgrn_lower_bounds, l3_w_in, l3_norm_g, l3_w_out,
                    l3_ln_g, l3_ln_b, layer=3, in_tb=True, out_tb=False)
    return h.reshape(BATCH, SEQ, D_MODEL)
```

```python
import functools
import math

import numpy as np
import jax
import jax.numpy as jnp
from jax import lax
from jax.experimental import pallas as pl
from jax.experimental.pallas import tpu as pltpu

D_MODEL = 1024
BATCH = 16
SEQ = 2048
DEPTH = 4
HEADS = 8
HEAD_DIM = 128
MOBA_BLOCK = 256
MOBA_TOPK = 3
ROPE_THETA = 10000.0
S5_GROUP = 16
S5_GROUPS = 64
S5_STATE = 64
ALPHA = (2 * DEPTH) ** 0.25
LN_EPS = 1e-5
RMS_EPS = 1e-6
NEG = -1e30

VMEM_LIMIT_BYTES = 56 * 1024 * 1024

HGRN_TILE = 256
HGRN_CHUNK = 128
PROJ_TILE = 512
S5_STEPS = 32
S5_SLAB = 256
S5_NSLAB = D_MODEL // S5_SLAB
S5_SLAB_STATES = (S5_SLAB // S5_GROUP) * S5_STATE

F32 = jnp.float32
BF16 = jnp.bfloat16


def _dot(a, b):
    return jnp.dot(a, b, preferred_element_type=F32)


def _dot_nt(a, b):
    return lax.dot_general(a, b, (((1,), (1,)), ((), ())), preferred_element_type=F32)


def _split3(a):
    hi = a.astype(BF16)
    r1 = a - hi.astype(F32)
    mid = r1.astype(BF16)
    lo = (r1 - mid.astype(F32)).astype(BF16)
    return hi, mid, lo


def _residual_layer_norm(x, y, g, b):
    r = ALPHA * x + y
    mu = jnp.mean(r, axis=-1, keepdims=True)
    c = r - mu
    var = jnp.mean(c * c, axis=-1, keepdims=True)
    return c * lax.rsqrt(var + LN_EPS) * g + b


def _const_spec(shape):
    nd = len(shape)
    return pl.BlockSpec(shape, lambda *_: (0,) * nd, pipeline_mode=pl.Buffered(1))


def _hgrn_level_table(c):
    t = np.arange(c)[:, None]
    s = np.arange(c)[None, :]
    x = np.bitwise_xor(t, s)
    lv = np.zeros((c, c), np.int32)
    nz = x > 0
    lv[nz] = 2 ** (np.floor(np.log2(x[nz])).astype(np.int32) + 1)
    lv[s >= t] = 0
    return lv


def _segment_mid_rows(bc, seg):
    c, w = bc.shape
    half = seg // 2
    if seg >= 8:
        pieces = [jnp.broadcast_to(bc[a + half - 1:a + half, :], (seg, w)) for a in range(0, c, seg)]
        return jnp.concatenate(pieces, axis=0)
    row = lax.broadcasted_iota(jnp.int32, (8, w), 0)
    pieces = []
    for a in range(0, c, 8):
        cand = [jnp.broadcast_to(bc[a + o + half - 1:a + o + half, :], (8, w)) for o in range(0, 8, seg)]
        out = cand[-1]
        for idx in range(len(cand) - 2, -1, -1):
            out = jnp.where(row < (idx + 1) * seg, cand[idx], out)
        pieces.append(out)
    return jnp.concatenate(pieces, axis=0)


def _hgrn_kernel(layer, x_ref, lbl_ref, lv_ref, tri_ref, w_in_ref, ng_ref, w_out_ref, lng_ref, lnb_ref,
                 out_ref, st_ref, o_ref):
    c = HGRN_CHUNK

    @pl.when(pl.program_id(1) == 0)
    def _():
        st_ref[...] = jnp.zeros_like(st_ref)

    x = x_ref[...]
    z = _dot(x.astype(BF16), w_in_ref[...])

    lbl = lbl_ref[...]
    e = jnp.exp(lbl - jnp.max(lbl, axis=0, keepdims=True))
    sm = e / jnp.sum(e, axis=0, keepdims=True)
    lb = jnp.zeros((1, D_MODEL), F32)
    for r in range(1, layer + 1):
        lb = lb + sm[r:r + 1, :]

    q_all = jax.nn.silu(z[:, 0:D_MODEL])
    f_all = lb + (1.0 - lb) * jax.nn.sigmoid(z[:, D_MODEL:2 * D_MODEL])
    g_all = jnp.log(f_all)
    k_all = 1.0 - f_all
    v_all = z[:, 2 * D_MODEL:3 * D_MODEL]
    gate_all = jax.nn.silu(z[:, 3 * D_MODEL:4 * D_MODEL])

    lv = lv_ref[...]
    tri = tri_ref[...]
    ng = ng_ref[...]
    row_in_chunk = lax.broadcasted_iota(jnp.int32, (c, D_MODEL), 0)

    for ci in range(HGRN_TILE // c):
        r0 = ci * c
        q = q_all[r0:r0 + c]
        k = k_all[r0:r0 + c]
        v = v_all[r0:r0 + c]
        g = g_all[r0:r0 + c]
        g_hi, g_mid, g_lo = _split3(g)
        bc = _dot(tri, g_hi) + _dot(tri, g_mid) + _dot(tri, g_lo)

        w_levels = []
        seg = c
        while seg >= 2:
            mid = _segment_mid_rows(bc, seg)
            decay = jnp.exp(-jnp.abs(bc - mid))
            second = (row_in_chunk & (seg - 1)) >= (seg // 2)
            w_levels.append((seg, (jnp.where(second, q, k) * decay).astype(BF16)))
            seg //= 2

        qd = (q * jnp.exp(bc)).astype(BF16)
        b_last = bc[c - 1:c, :]
        kd = (k * jnp.exp(b_last - bc)).astype(BF16)
        d_last = jnp.exp(b_last)
        qk = q * k
        v_bf = v.astype(BF16)

        for h in range(HEADS):
            sl = slice(h * HEAD_DIM, (h + 1) * HEAD_DIM)
            scores = jnp.zeros((c, c), F32)
            for seg, w in w_levels:
                wh = w[:, sl]
                scores = jnp.where(lv == seg, _dot_nt(wh, wh), scores)
            st = st_ref[h]
            o = _dot(scores.astype(BF16), v_bf[:, sl])
            o = o + jnp.sum(qk[:, sl], axis=-1, keepdims=True) * v[:, sl]
            o = o + _dot_nt(qd[:, sl], st.astype(BF16))
            st_ref[h] = st * d_last[:, sl] + _dot(v[:, sl].T.astype(BF16), kd[:, sl])
            o = o * lax.rsqrt(jnp.mean(o * o, axis=-1, keepdims=True) + RMS_EPS) * ng[:, sl]
            o_ref[r0:r0 + c, sl] = (o * gate_all[r0:r0 + c, sl]).astype(BF16)

    y = _dot(o_ref[...], w_out_ref[...])
    out_ref[...] = _residual_layer_norm(x, y, lng_ref[...], lnb_ref[...])


def _hgrn_layer(x2d, lb_logits, w_in, norm_g, w_out, ln_g, ln_b, *, layer, in_tb, out_tb):
    nt = SEQ // HGRN_TILE
    tile = (HGRN_TILE, D_MODEL)
    tb_spec = pl.BlockSpec(tile, lambda b, c: (c, b))
    bt_spec = pl.BlockSpec(tile, lambda b, c: (b * nt + c, 0))
    out_shape = (SEQ, BATCH * D_MODEL) if out_tb else (BATCH * SEQ, D_MODEL)
    lv = jnp.asarray(_hgrn_level_table(HGRN_CHUNK))
    tri = jnp.asarray(np.tril(np.ones((HGRN_CHUNK, HGRN_CHUNK), np.float32)), dtype=BF16)
    return pl.pallas_call(
        functools.partial(_hgrn_kernel, layer),
        grid=(BATCH, nt),
        in_specs=[
            tb_spec if in_tb else bt_spec,
            _const_spec((DEPTH, D_MODEL)),
            _const_spec((HGRN_CHUNK, HGRN_CHUNK)),
            _const_spec((HGRN_CHUNK, HGRN_CHUNK)),
            _const_spec((D_MODEL, 4 * D_MODEL)),
            _const_spec((1, D_MODEL)),
            _const_spec((D_MODEL, D_MODEL)),
            _const_spec((1, D_MODEL)),
            _const_spec((1, D_MODEL)),
        ],
        out_specs=tb_spec if out_tb else bt_spec,
        out_shape=jax.ShapeDtypeStruct(out_shape, F32),
        scratch_shapes=[
            pltpu.VMEM((HEADS, HEAD_DIM, HEAD_DIM), F32),
            pltpu.VMEM((HGRN_TILE, D_MODEL), BF16),
        ],
        compiler_params=pltpu.CompilerParams(
            dimension_semantics=("arbitrary", "arbitrary"),
            vmem_limit_bytes=VMEM_LIMIT_BYTES),
        name=f"hgrn2_layer{layer}",
    )(x2d, lb_logits, lv, tri, w_in.astype(BF16), norm_g.reshape(1, D_MODEL), w_out.astype(BF16),
      ln_g.reshape(1, D_MODEL), ln_b.reshape(1, D_MODEL))


def _moba_proj_kernel(x_ref, w_in_ref, cq_ref, sq_ref, ck_ref, sk_ref, q_ref, k_ref, vt_ref, gate_ref):
    z = _dot(x_ref[...].astype(BF16), w_in_ref[...])
    cq, sq, ck, sk = cq_ref[...], sq_ref[...], ck_ref[...], sk_ref[...]
    for h in range(HEADS):
        qh = z[:, h * HEAD_DIM:(h + 1) * HEAD_DIM]
        kh = z[:, D_MODEL + h * HEAD_DIM:D_MODEL + (h + 1) * HEAD_DIM]
        vh = z[:, 2 * D_MODEL + h * HEAD_DIM:2 * D_MODEL + (h + 1) * HEAD_DIM]
        q_ref[0, h] = (qh * cq + pltpu.roll(qh, HEAD_DIM // 2, 1) * sq).astype(BF16)
        k_ref[0, h] = (kh * ck + pltpu.roll(kh, HEAD_DIM // 2, 1) * sk).astype(BF16)
        vt_ref[0, h] = vh.T.astype(BF16)
    gate_ref[...] = jax.nn.silu(z[:, 3 * D_MODEL:4 * D_MODEL]).astype(BF16)


def _moba_attn_kernel(q_ref, k_ref, vt_ref, gate_ref, x_ref, w_out_ref, lng_ref, lnb_ref,
                      out_ref, km_ref, o_ref):
    n = pl.program_id(1)
    blk = MOBA_BLOCK
    nb = SEQ // blk

    @pl.when(n == 0)
    def _():
        for h in range(HEADS):
            kf = k_ref[0, h].astype(F32).reshape(nb, blk, HEAD_DIM)
            km_ref[h] = jnp.mean(kf, axis=1)

    key = lax.broadcasted_iota(jnp.int32, (blk, blk), 0)
    qry = lax.broadcasted_iota(jnp.int32, (blk, blk), 1)
    causal = key <= qry

    def attend(n_past):
        n_keys = (n_past + 1) * blk

        def head_body(h, carry):
            q = q_ref[0, h]
            s = _dot_nt(k_ref[0, h, 0:n_keys, :], q)
            if n_past > MOBA_TOPK:
                km = km_ref[h]
                km_hi = km.astype(BF16)
                km_lo = (km - km_hi.astype(F32)).astype(BF16)
                gsc = _dot_nt(km_hi, q) + _dot_nt(km_lo, q)
                g = [gsc[j:j + 1, :] for j in range(n_past)]
            parts = []
            for j in range(n_past):
                sj = s[j * blk:(j + 1) * blk]
                if n_past > MOBA_TOPK:
                    cnt = jnp.zeros((1, blk), F32)
                    for i in range(n_past):
                        if i != j:
                            beats = (g[i] >= g[j]) if i < j else (g[i] > g[j])
                            cnt = cnt + beats.astype(F32)
                    sj = jnp.where(cnt < MOBA_TOPK, sj, NEG)
                parts.append(sj)
            parts.append(jnp.where(causal, s[n_past * blk:], NEG))
            s = jnp.concatenate(parts, axis=0)
            m = jnp.max(s, axis=0, keepdims=True)
            p = jnp.exp(s - m)
            l = jnp.sum(p, axis=0, keepdims=True)
            acc = _dot(vt_ref[0, h, :, 0:n_keys], p.astype(BF16))
            o_ref[h] = (acc * (1.0 / l)).T
            return carry

        lax.fori_loop(0, HEADS, head_body, 0, unroll=2)

    for n_past in range(nb):
        pl.when(n == n_past)(functools.partial(attend, n_past))

    o = jnp.concatenate([o_ref[h] for h in range(HEADS)], axis=1)
    o = (o * gate_ref[...].astype(F32)).astype(BF16)
    y = _dot(o, w_out_ref[...])
    out_ref[...] = _residual_layer_norm(x_ref[...], y, lng_ref[...], lnb_ref[...])


def _rope_tables():
    pos = np.arange(SEQ, dtype=np.float32)
    inv_freq = (1.0 / (np.float32(ROPE_THETA) ** (np.arange(0, HEAD_DIM, 2, dtype=np.float32) / HEAD_DIM))
                ).astype(np.float32)
    ang = pos[:, None] * inv_freq[None, :]
    cos, sin = np.cos(ang).astype(np.float32), np.sin(ang).astype(np.float32)
    c = np.concatenate([cos, cos], axis=1)
    s = np.concatenate([-sin, sin], axis=1)
    scale = np.float32(HEAD_DIM ** -0.5)
    return c * scale, s * scale, c, s


def _moba_layer(x2d, w_in, w_out, ln_g, ln_b):
    tiles_per_seq = SEQ // PROJ_TILE
    hshape = (BATCH, HEADS, SEQ, HEAD_DIM)
    hspec = pl.BlockSpec((1, HEADS, PROJ_TILE, HEAD_DIM), lambda i: (i // tiles_per_seq, 0, i % tiles_per_seq, 0))
    tspec = pl.BlockSpec((PROJ_TILE, HEAD_DIM), lambda i: (i % tiles_per_seq, 0))
    nb = SEQ // MOBA_BLOCK
    vt_shape = (BATCH, HEADS, HEAD_DIM, SEQ)
    vt_spec = pl.BlockSpec((1, HEADS, HEAD_DIM, PROJ_TILE),
                           lambda i: (i // tiles_per_seq, 0, 0, i % tiles_per_seq))
    cq, sq, ck, sk = (jnp.asarray(t) for t in _rope_tables())
    q, k, vt, gate = pl.pallas_call(
        _moba_proj_kernel,
        grid=(BATCH * tiles_per_seq,),
        in_specs=[
            pl.BlockSpec((PROJ_TILE, D_MODEL), lambda i: (i, 0)),
            _const_spec((D_MODEL, 4 * D_MODEL)),
            tspec, tspec, tspec, tspec,
        ],
        out_specs=[hspec, hspec, vt_spec, pl.BlockSpec((PROJ_TILE, D_MODEL), lambda i: (i, 0))],
        out_shape=[jax.ShapeDtypeStruct(hshape, BF16)] * 2 + [jax.ShapeDtypeStruct(vt_shape, BF16),
                                                              jax.ShapeDtypeStruct((BATCH * SEQ, D_MODEL), BF16)],
        compiler_params=pltpu.CompilerParams(
            dimension_semantics=("arbitrary",), vmem_limit_bytes=VMEM_LIMIT_BYTES),
        name="moba_proj",
    )(x2d, w_in.astype(BF16), cq, sq, ck, sk)

    k_spec = pl.BlockSpec((1, HEADS, SEQ, HEAD_DIM), lambda b, n: (b, 0, 0, 0))
    vt_in_spec = pl.BlockSpec((1, HEADS, HEAD_DIM, SEQ), lambda b, n: (b, 0, 0, 0))
    row_spec = pl.BlockSpec((MOBA_BLOCK, D_MODEL), lambda b, n: (b * nb + n, 0))
    return pl.pallas_call(
        _moba_attn_kernel,
        grid=(BATCH, nb),
        in_specs=[
            pl.BlockSpec((1, HEADS, MOBA_BLOCK, HEAD_DIM), lambda b, n: (b, 0, n, 0)),
            k_spec, vt_in_spec, row_spec, row_spec,
            _const_spec((D_MODEL, D_MODEL)),
            _const_spec((1, D_MODEL)),
            _const_spec((1, D_MODEL)),
        ],
        out_specs=pl.BlockSpec((MOBA_BLOCK, D_MODEL), lambda b, n: (n, b)),
        out_shape=jax.ShapeDtypeStruct((SEQ, BATCH * D_MODEL), F32),
        scratch_shapes=[
            pltpu.VMEM((HEADS, nb, HEAD_DIM), F32),
            pltpu.VMEM((HEADS, MOBA_BLOCK, HEAD_DIM), F32),
        ],
        compiler_params=pltpu.CompilerParams(
            dimension_semantics=("arbitrary", "arbitrary"), vmem_limit_bytes=VMEM_LIMIT_BYTES),
        name="moba_attn",
    )(q, k, vt, gate, x2d, w_out.astype(BF16), ln_g.reshape(1, D_MODEL), ln_b.reshape(1, D_MODEL))


def _s5_discretize_kernel(ar_ref, ai_ref, ldt_ref, br_ref, bi_ref, abr_ref, abi_ref, bbr_ref, bbi_ref):
    ar, ai = ar_ref[...], ai_ref[...]
    dt = jnp.exp(ldt_ref[...])
    mag = jnp.exp(dt * ar)
    abr = mag * jnp.cos(dt * ai)
    abi = mag * jnp.sin(dt * ai)
    nr, ni = abr - 1.0, abi
    den = ar * ar + ai * ai
    zr = (nr * ar + ni * ai) / den
    zi = (ni * ar - nr * ai) / den
    br, bi = br_ref[...], bi_ref[...]
    abr_ref[...] = abr
    abi_ref[...] = abi
    bbr_ref[...] = zr * br - zi * bi
    bbi_ref[...] = zr * bi + zi * br


def _s5_kernel(x_ref, w_in_ref, bd_ref, lam_ref, cd_ref, d_ref, w_glu_ref, b_glu_ref, w_out_ref,
               lng_ref, lnb_ref, out_ref, hst_ref, xs_ref, hb_ref, y_ref):
    ns = S5_SLAB_STATES

    @pl.when(pl.program_id(0) == 0)
    def _():
        hst_ref[...] = jnp.zeros_like(hst_ref)

    x = x_ref[...]
    z = _dot(x.astype(BF16), w_in_ref[...])
    u = z[:, 0:D_MODEL]
    gate = z[:, D_MODEL:2 * D_MODEL]
    u_bf = u.astype(BF16)

    for s in range(S5_NSLAB):
        xs_ref[...] = _dot(u_bf[:, s * S5_SLAB:(s + 1) * S5_SLAB], bd_ref[s])
        lr = jnp.broadcast_to(lam_ref[s, 0:1, :], (BATCH, ns))
        li = jnp.broadcast_to(lam_ref[s, 1:2, :], (BATCH, ns))

        def step(t, h):
            hr, hi = h
            r = pl.multiple_of(t * BATCH, BATCH)
            xr = xs_ref[pl.ds(r, BATCH), 0:ns]
            xi = xs_ref[pl.ds(r, BATCH), ns:2 * ns]
            nhr = lr * hr - li * hi + xr
            nhi = lr * hi + li * hr + xi
            hb_ref[pl.ds(r, BATCH), 0:ns] = nhr.astype(BF16)
            hb_ref[pl.ds(r, BATCH), ns:2 * ns] = nhi.astype(BF16)
            return nhr, nhi

        hr, hi = lax.fori_loop(0, S5_STEPS, step, (hst_ref[s, 0], hst_ref[s, 1]))
        hst_ref[s, 0] = hr
        hst_ref[s, 1] = hi
        y_ref[:, s * S5_SLAB:(s + 1) * S5_SLAB] = _dot(hb_ref[...], cd_ref[s])

    y = y_ref[...] + d_ref[...] * u
    y = jax.nn.gelu(y)
    y = y * jax.nn.sigmoid(_dot(y.astype(BF16), w_glu_ref[...]) + b_glu_ref[...])
    y = y * jax.nn.silu(gate)
    out = _dot(y.astype(BF16), w_out_ref[...])
    out_ref[...] = _residual_layer_norm(x, out, lng_ref[...], lnb_ref[...])


def _s5_layer(x_tb, w_in, a_re, a_im, log_dt, b_re, b_im, c_re, c_im, d_skip, w_glu, b_glu, w_out, ln_g, ln_b):
    gh = S5_GROUPS * S5_GROUP
    rep = lambda a: jnp.repeat(a, S5_GROUP, axis=0)
    small = jax.ShapeDtypeStruct((gh, S5_STATE), F32)
    abr, abi, bbr, bbi = pl.pallas_call(
        _s5_discretize_kernel,
        out_shape=[small] * 4,
        name="s5_discretize",
    )(rep(a_re), rep(a_im), rep(jnp.broadcast_to(log_dt[:, None], (S5_GROUPS, S5_STATE))),
      b_re.transpose(0, 2, 1).reshape(gh, S5_STATE), b_im.transpose(0, 2, 1).reshape(gh, S5_STATE))

    gl = S5_SLAB // S5_GROUP
    eye = jnp.eye(gl, dtype=F32)

    def block_diag_in(bb):
        t = bb.reshape(S5_NSLAB, gl, S5_GROUP, S5_STATE)
        return jnp.einsum('sghp,gk->sghkp', t, eye).reshape(S5_NSLAB, S5_SLAB, gl * S5_STATE)

    def block_diag_out(cc):
        t = cc.reshape(S5_NSLAB, gl, S5_GROUP, S5_STATE)
        return jnp.einsum('sghp,gk->skpgh', t, eye).reshape(S5_NSLAB, gl * S5_STATE, S5_SLAB)

    bd = jnp.concatenate([block_diag_in(bbr), block_diag_in(bbi)], axis=2).astype(BF16)
    cd = jnp.concatenate([block_diag_out(c_re), -block_diag_out(c_im)], axis=1).astype(BF16)
    lam = jnp.stack([abr[::S5_GROUP].reshape(S5_NSLAB, gl * S5_STATE),
                     abi[::S5_GROUP].reshape(S5_NSLAB, gl * S5_STATE)], axis=1)

    rows = S5_STEPS * BATCH
    ns2 = 2 * S5_SLAB_STATES
    row_spec = pl.BlockSpec((rows, D_MODEL), lambda i: (i, 0))
    return pl.pallas_call(
        _s5_kernel,
        grid=(SEQ // S5_STEPS,),
        in_specs=[
            row_spec,
            _const_spec((D_MODEL, 2 * D_MODEL)),
            _const_spec((S5_NSLAB, S5_SLAB, ns2)),
            _const_spec((S5_NSLAB, 2, S5_SLAB_STATES)),
            _const_spec((S5_NSLAB, ns2, S5_SLAB)),
            _const_spec((1, D_MODEL)),
            _const_spec((D_MODEL, D_MODEL)),
            _const_spec((1, D_MODEL)),
            _const_spec((D_MODEL, D_MODEL)),
            _const_spec((1, D_MODEL)),
            _const_spec((1, D_MODEL)),
        ],
        out_specs=row_spec,
        out_shape=jax.ShapeDtypeStruct((SEQ * BATCH, D_MODEL), F32),
        scratch_shapes=[
            pltpu.VMEM((S5_NSLAB, 2, BATCH, S5_SLAB_STATES), F32),
            pltpu.VMEM((rows, ns2), F32),
            pltpu.VMEM((rows, ns2), BF16),
            pltpu.VMEM((rows, D_MODEL), F32),
        ],
        compiler_params=pltpu.CompilerParams(
            dimension_semantics=("arbitrary",), vmem_limit_bytes=VMEM_LIMIT_BYTES),
        name="s5_layer",
    )(x_tb, w_in.astype(BF16), bd, lam, cd, d_skip.reshape(1, D_MODEL), w_glu.astype(BF16),
      b_glu.reshape(1, D_MODEL), w_out.astype(BF16), ln_g.reshape(1, D_MODEL), ln_b.reshape(1, D_MODEL))


def kernel(x, hgrn_lower_bounds,
           l0_w_in, l0_norm_g, l0_w_out, l0_ln_g, l0_ln_b,
           l1_w_in, l1_w_out, l1_ln_g, l1_ln_b,
           l2_w_in, l2_a_re, l2_a_im, l2_log_dt, l2_b_re, l2_b_im, l2_c_re, l2_c_im,
           l2_d, l2_w_glu, l2_b_glu, l2_w_out, l2_ln_g, l2_ln_b,
           l3_w_in, l3_norm_g, l3_w_out, l3_ln_g, l3_ln_b):
    h = x.reshape(BATCH * SEQ, D_MODEL)
    h = _hgrn_layer(h, hgrn_lower_bounds, l0_w_in, l0_norm_g, l0_w_out, l0_ln_g, l0_ln_b,
                    layer=0, in_tb=False, out_tb=False)
    h = _moba_layer(h, l1_w_in, l1_w_out, l1_ln_g, l1_ln_b)
    h = _s5_layer(h.reshape(SEQ * BATCH, D_MODEL), l2_w_in, l2_a_re, l2_a_im, l2_log_dt, l2_b_re, l2_b_im,
                  l2_c_re, l2_c_im, l2_d, l2_w_glu, l2_b_glu, l2_w_out, l2_ln_g, l2_ln_b)
    h = _hgrn_layer(h.reshape(SEQ, BATCH * D_MODEL), hgrn_lower_bounds, l3_w_in, l3_norm_g, l3_w_out,
                    l3_ln_g, l3_ln_b, layer=3, in_tb=True, out_tb=False)
    return h.reshape(BATCH, SEQ, D_MODEL)
```

```python
import functools
import math

import numpy as np
import jax
import jax.numpy as jnp
from jax import lax
from jax.experimental import pallas as pl
from jax.experimental.pallas import tpu as pltpu

D_MODEL = 1024
BATCH = 16
SEQ = 2048
DEPTH = 4
HEADS = 8
HEAD_DIM = 128
MOBA_BLOCK = 256
MOBA_TOPK = 3
ROPE_THETA = 10000.0
S5_GROUP = 16
S5_GROUPS = 64
S5_STATE = 64
ALPHA = (2 * DEPTH) ** 0.25
LN_EPS = 1e-5
RMS_EPS = 1e-6
NEG = -1e30

VMEM_LIMIT_BYTES = 56 * 1024 * 1024

HGRN_TILE = 256
HGRN_CHUNK = 128
PROJ_TILE = 512
S5_STEPS = 32
S5_SLAB = 256
S5_NSLAB = D_MODEL // S5_SLAB
S5_SLAB_STATES = (S5_SLAB // S5_GROUP) * S5_STATE

F32 = jnp.float32
BF16 = jnp.bfloat16


def _dot(a, b):
    return jnp.dot(a, b, preferred_element_type=F32)


def _dot_nt(a, b):
    return lax.dot_general(a, b, (((1,), (1,)), ((), ())), preferred_element_type=F32)


def _split3(a):
    hi = a.astype(BF16)
    r1 = a - hi.astype(F32)
    mid = r1.astype(BF16)
    lo = (r1 - mid.astype(F32)).astype(BF16)
    return hi, mid, lo


def _residual_layer_norm(x, y, g, b):
    r = ALPHA * x + y
    mu = jnp.mean(r, axis=-1, keepdims=True)
    c = r - mu
    var = jnp.mean(c * c, axis=-1, keepdims=True)
    return c * lax.rsqrt(var + LN_EPS) * g + b


def _const_spec(shape):
    nd = len(shape)
    return pl.BlockSpec(shape, lambda *_: (0,) * nd, pipeline_mode=pl.Buffered(1))


def _hgrn_level_table(c):
    t = np.arange(c)[:, None]
    s = np.arange(c)[None, :]
    x = np.bitwise_xor(t, s)
    lv = np.zeros((c, c), np.int32)
    nz = x > 0
    lv[nz] = 2 ** (np.floor(np.log2(x[nz])).astype(np.int32) + 1)
    lv[s >= t] = 0
    return lv


def _segment_mid_rows(bc, seg):
    c, w = bc.shape
    half = seg // 2
    row = lax.broadcasted_iota(jnp.int32, (8, w), 0)
    pieces = []
    for a in range(0, c, 8):
        cand = [jnp.broadcast_to(bc[a + o + half - 1:a + o + half, :], (8, w)) for o in range(0, 8, seg)]
        out = cand[-1]
        for idx in range(len(cand) - 2, -1, -1):
            out = jnp.where(row < (idx + 1) * seg, cand[idx], out)
        pieces.append(out)
    return jnp.concatenate(pieces, axis=0)


def _level_operand(q, k, bc, seg, second):
    c, _ = bc.shape
    half = seg // 2
    if half >= 8:
        pieces = []
        for a in range(0, c, seg):
            m = bc[a + half - 1:a + half, :]
            pieces.append(k[a:a + half] * jnp.exp2(m - bc[a:a + half]))
            pieces.append(q[a + half:a + seg] * jnp.exp2(bc[a + half:a + seg] - m))
        return jnp.concatenate(pieces, axis=0)
    decay = jnp.exp2(-jnp.abs(bc - _segment_mid_rows(bc, seg)))
    return jnp.where(second, q, k) * decay


HGRN_PROJ_PIECE = 256


def _hgrn_kernel(layer, x_ref, xn_ref, lbl_ref, lv_ref, tri_ref, w_in_ref, ng_ref, w_out_ref, lng_ref, lnb_ref,
                 out_ref, st_ref, o_ref, xb_ref, za_ref, zb_ref):
    t = HGRN_TILE

    def proj_pieces(src, slot, z_ref):
        def cast():
            xb_ref[slot] = src[...].astype(BF16)

        def piece(j):
            cols = slice(j * HGRN_PROJ_PIECE, (j + 1) * HGRN_PROJ_PIECE)
            z_ref[:, cols] = _dot(xb_ref[slot], w_in_ref[:, cols])

        return [cast] + [functools.partial(piece, j) for j in range(4 * D_MODEL // HGRN_PROJ_PIECE)]

    @pl.when((pl.program_id(0) == 0) & (pl.program_id(1) == 0))
    def _():
        for job in proj_pieces(x_ref.at[0:t, :], 0, za_ref):
            job()

    @pl.when(pl.program_id(1) == 0)
    def _():
        st_ref[...] = jnp.zeros_like(st_ref)

    lbl = lbl_ref[...]
    e = jnp.exp(lbl - jnp.max(lbl, axis=0, keepdims=True))
    sm = e / jnp.sum(e, axis=0, keepdims=True)
    lb = jnp.zeros((1, D_MODEL), F32)
    for r in range(1, layer + 1):
        lb = lb + sm[r:r + 1, :]

    mix = functools.partial(_hgrn_mixer, lb, lv_ref, tri_ref, ng_ref, w_out_ref, lng_ref, lnb_ref, st_ref)
    out_ref[0:t, :] = mix(za_ref, x_ref.at[0:t, :], o_ref.at[0], proj_pieces(x_ref.at[t:2 * t, :], 0, zb_ref))
    out_ref[t:2 * t, :] = mix(zb_ref, x_ref.at[t:2 * t, :], o_ref.at[1], proj_pieces(xn_ref, 1, za_ref))


def _hgrn_mixer(lb, lv_ref, tri_ref, ng_ref, w_out_ref, lng_ref, lnb_ref, st_ref, z_ref, x_ref, o_ref, side_jobs):
    c = HGRN_CHUNK
    side_jobs = list(side_jobs)

    def run_side_job():
        if side_jobs:
            side_jobs.pop(0)()

    lv = lv_ref[...]
    tri = tri_ref[...]
    ng = ng_ref[...]
    row_in_chunk = lax.broadcasted_iota(jnp.int32, (c, D_MODEL), 0)

    for ci in range(HGRN_TILE // c):
        rows = slice(ci * c, (ci + 1) * c)
        q = jax.nn.silu(z_ref[rows, 0:D_MODEL])
        f = lb + (1.0 - lb) * jax.nn.sigmoid(z_ref[rows, D_MODEL:2 * D_MODEL])
        g = jnp.log2(f)
        k = 1.0 - f
        v = z_ref[rows, 2 * D_MODEL:3 * D_MODEL]
        gate = jax.nn.silu(z_ref[rows, 3 * D_MODEL:4 * D_MODEL])
        run_side_job()
        g_hi, g_mid, g_lo = _split3(g)
        bc = _dot(tri, g_hi) + _dot(tri, g_mid) + _dot(tri, g_lo)
        run_side_job()

        w_levels = []
        seg = c
        while seg > 2:
            second = (row_in_chunk & (seg - 1)) >= (seg // 2)
            w_levels.append((seg, _level_operand(q, k, bc, seg, second).astype(BF16)))
            seg //= 2
        odd = (row_in_chunk & 1) == 1
        w_levels.append((2, jnp.where(odd, q * f, k).astype(BF16)))

        qd = (q * jnp.exp2(bc)).astype(BF16)
        b_last = bc[c - 1:c, :]
        kd = (k * jnp.exp2(b_last - bc)).astype(BF16)
        d_last = jnp.exp2(b_last)
        qk = q * k
        v_bf = v.astype(BF16)
        run_side_job()

        for h in range(HEADS):
            sl = slice(h * HEAD_DIM, (h + 1) * HEAD_DIM)
            scores = jnp.zeros((c, c), F32)
            for seg, w in w_levels:
                wh = w[:, sl]
                scores = jnp.where(lv == seg, _dot_nt(wh, wh), scores)
            st = st_ref[h]
            o = _dot(scores.astype(BF16), v_bf[:, sl])
            o = o + jnp.sum(qk[:, sl], axis=-1, keepdims=True) * v[:, sl]
            o = o + _dot_nt(qd[:, sl], st.astype(BF16))
            st_ref[h] = st * d_last[:, sl] + _dot(v[:, sl].T.astype(BF16), kd[:, sl])
            o = o * lax.rsqrt(jnp.mean(o * o, axis=-1, keepdims=True) + RMS_EPS) * ng[:, sl]
            o_ref[rows, sl] = (o * gate[:, sl]).astype(BF16)
            run_side_job()

    while side_jobs:
        run_side_job()
    y = _dot(o_ref[...], w_out_ref[...])
    return _residual_layer_norm(x_ref[...], y, lng_ref[...], lnb_ref[...])


def _hgrn_layer(x2d, lb_logits, w_in, norm_g, w_out, ln_g, ln_b, *, layer, in_tb, out_tb):
    nt = SEQ // HGRN_TILE
    ns = nt // 2
    pair = (2 * HGRN_TILE, D_MODEL)
    tile = (HGRN_TILE, D_MODEL)
    tb_spec = pl.BlockSpec(pair, lambda b, c: (c, b))
    bt_spec = pl.BlockSpec(pair, lambda b, c: (b * ns + c, 0))

    def next_tile(b, c):
        return jnp.minimum(2 * (b * ns + c) + 2, BATCH * nt - 1)

    tb_next = pl.BlockSpec(tile, lambda b, c: (next_tile(b, c) % nt, next_tile(b, c) // nt))
    bt_next = pl.BlockSpec(tile, lambda b, c: (next_tile(b, c), 0))
    out_shape = (SEQ, BATCH * D_MODEL) if out_tb else (BATCH * SEQ, D_MODEL)
    lv = jnp.asarray(_hgrn_level_table(HGRN_CHUNK))
    tri = jnp.asarray(np.tril(np.ones((HGRN_CHUNK, HGRN_CHUNK), np.float32)), dtype=BF16)
    return pl.pallas_call(
        functools.partial(_hgrn_kernel, layer),
        grid=(BATCH, ns),
        in_specs=[
            tb_spec if in_tb else bt_spec,
            tb_next if in_tb else bt_next,
            _const_spec((DEPTH, D_MODEL)),
            _const_spec((HGRN_CHUNK, HGRN_CHUNK)),
            _const_spec((HGRN_CHUNK, HGRN_CHUNK)),
            _const_spec((D_MODEL, 4 * D_MODEL)),
            _const_spec((1, D_MODEL)),
            _const_spec((D_MODEL, D_MODEL)),
            _const_spec((1, D_MODEL)),
            _const_spec((1, D_MODEL)),
        ],
        out_specs=tb_spec if out_tb else bt_spec,
        out_shape=jax.ShapeDtypeStruct(out_shape, F32),
        scratch_shapes=[
            pltpu.VMEM((HEADS, HEAD_DIM, HEAD_DIM), F32),
            pltpu.VMEM((2, HGRN_TILE, D_MODEL), BF16),
            pltpu.VMEM((2, HGRN_TILE, D_MODEL), BF16),
            pltpu.VMEM((HGRN_TILE, 4 * D_MODEL), F32),
            pltpu.VMEM((HGRN_TILE, 4 * D_MODEL), F32),
        ],
        compiler_params=pltpu.CompilerParams(
            dimension_semantics=("arbitrary", "arbitrary"),
            vmem_limit_bytes=VMEM_LIMIT_BYTES),
        name=f"hgrn2_layer{layer}",
    )(x2d, x2d, lb_logits, lv, tri, w_in.astype(BF16), norm_g.reshape(1, D_MODEL), w_out.astype(BF16),
      ln_g.reshape(1, D_MODEL), ln_b.reshape(1, D_MODEL))


def _moba_proj_kernel(x_ref, w_in_ref, cq_ref, sq_ref, ck_ref, sk_ref, q_ref, k_ref, vt_ref, gate_ref):
    z = _dot(x_ref[...].astype(BF16), w_in_ref[...])
    cq, sq, ck, sk = cq_ref[...], sq_ref[...], ck_ref[...], sk_ref[...]
    for h in range(HEADS):
        qh = z[:, h * HEAD_DIM:(h + 1) * HEAD_DIM]
        kh = z[:, D_MODEL + h * HEAD_DIM:D_MODEL + (h + 1) * HEAD_DIM]
        vh = z[:, 2 * D_MODEL + h * HEAD_DIM:2 * D_MODEL + (h + 1) * HEAD_DIM]
        q_ref[0, h] = (qh * cq + pltpu.roll(qh, HEAD_DIM // 2, 1) * sq).astype(BF16)
        k_ref[0, h] = (kh * ck + pltpu.roll(kh, HEAD_DIM // 2, 1) * sk).astype(BF16)
        vt_ref[0, h] = vh.T.astype(BF16)
    gate_ref[...] = jax.nn.silu(z[:, 3 * D_MODEL:4 * D_MODEL]).astype(BF16)


def _moba_attn_kernel(q_ref, k_ref, vt_ref, gate_ref, x_ref, w_out_ref, lng_ref, lnb_ref,
                      out_ref, km_ref, o_ref):
    n = pl.program_id(1)
    blk = MOBA_BLOCK
    nb = SEQ // blk

    @pl.when(n == 0)
    def _():
        for h in range(HEADS):
            kf = k_ref[0, h].astype(F32).reshape(nb, blk, HEAD_DIM)
            km_ref[h] = jnp.mean(kf, axis=1)

    key = lax.broadcasted_iota(jnp.int32, (blk, blk), 0)
    qry = lax.broadcasted_iota(jnp.int32, (blk, blk), 1)
    causal = key <= qry

    def attend(n_past):
        n_keys = (n_past + 1) * blk

        def head_body(h, carry):
            q = q_ref[0, h]
            s = _dot_nt(k_ref[0, h, 0:n_keys, :], q)
            if n_past > MOBA_TOPK:
                km = km_ref[h]
                km_hi = km.astype(BF16)
                km_lo = (km - km_hi.astype(F32)).astype(BF16)
                gsc = _dot_nt(km_hi, q) + _dot_nt(km_lo, q)
                g = [gsc[j:j + 1, :] for j in range(n_past)]
            parts = []
            for j in range(n_past):
                sj = s[j * blk:(j + 1) * blk]
                if n_past > MOBA_TOPK:
                    cnt = jnp.zeros((1, blk), F32)
                    for i in range(n_past):
                        if i != j:
                            beats = (g[i] >= g[j]) if i < j else (g[i] > g[j])
                            cnt = cnt + beats.astype(F32)
                    sj = jnp.where(cnt < MOBA_TOPK, sj, NEG)
                parts.append(sj)
            parts.append(jnp.where(causal, s[n_past * blk:], NEG))
            s = jnp.concatenate(parts, axis=0)
            m = jnp.max(s, axis=0, keepdims=True)
            p = jnp.exp(s - m)
            l = jnp.sum(p, axis=0, keepdims=True)
            acc = _dot(vt_ref[0, h, :, 0:n_keys], p.astype(BF16))
            o_ref[h] = (acc * (1.0 / l)).T
            return carry

        lax.fori_loop(0, HEADS, head_body, 0, unroll=2)

    for n_past in range(nb):
        pl.when(n == n_past)(functools.partial(attend, n_past))

    o = jnp.concatenate([o_ref[h] for h in range(HEADS)], axis=1)
    o = (o * gate_ref[...].astype(F32)).astype(BF16)
    y = _dot(o, w_out_ref[...])
    out_ref[...] = _residual_layer_norm(x_ref[...], y, lng_ref[...], lnb_ref[...])


def _rope_tables():
    pos = np.arange(SEQ, dtype=np.float32)
    inv_freq = (1.0 / (np.float32(ROPE_THETA) ** (np.arange(0, HEAD_DIM, 2, dtype=np.float32) / HEAD_DIM))
                ).astype(np.float32)
    ang = pos[:, None] * inv_freq[None, :]
    cos, sin = np.cos(ang).astype(np.float32), np.sin(ang).astype(np.float32)
    c = np.concatenate([cos, cos], axis=1)
    s = np.concatenate([-sin, sin], axis=1)
    scale = np.float32(HEAD_DIM ** -0.5)
    return c * scale, s * scale, c, s


def _moba_layer(x2d, w_in, w_out, ln_g, ln_b):
    tiles_per_seq = SEQ // PROJ_TILE
    hshape = (BATCH, HEADS, SEQ, HEAD_DIM)
    hspec = pl.BlockSpec((1, HEADS, PROJ_TILE, HEAD_DIM), lambda i: (i // tiles_per_seq, 0, i % tiles_per_seq, 0))
    tspec = pl.BlockSpec((PROJ_TILE, HEAD_DIM), lambda i: (i % tiles_per_seq, 0))
    nb = SEQ // MOBA_BLOCK
    vt_shape = (BATCH, HEADS, HEAD_DIM, SEQ)
    vt_spec = pl.BlockSpec((1, HEADS, HEAD_DIM, PROJ_TILE),
                           lambda i: (i // tiles_per_seq, 0, 0, i % tiles_per_seq))
    cq, sq, ck, sk = (jnp.asarray(t) for t in _rope_tables())
    q, k, vt, gate = pl.pallas_call(
        _moba_proj_kernel,
        grid=(BATCH * tiles_per_seq,),
        in_specs=[
            pl.BlockSpec((PROJ_TILE, D_MODEL), lambda i: (i, 0)),
            _const_spec((D_MODEL, 4 * D_MODEL)),
            tspec, tspec, tspec, tspec,
        ],
        out_specs=[hspec, hspec, vt_spec, pl.BlockSpec((PROJ_TILE, D_MODEL), lambda i: (i, 0))],
        out_shape=[jax.ShapeDtypeStruct(hshape, BF16)] * 2 + [jax.ShapeDtypeStruct(vt_shape, BF16),
                                                              jax.ShapeDtypeStruct((BATCH * SEQ, D_MODEL), BF16)],
        compiler_params=pltpu.CompilerParams(
            dimension_semantics=("arbitrary",), vmem_limit_bytes=VMEM_LIMIT_BYTES),
        name="moba_proj",
    )(x2d, w_in.astype(BF16), cq, sq, ck, sk)

    k_spec = pl.BlockSpec((1, HEADS, SEQ, HEAD_DIM), lambda b, n: (b, 0, 0, 0))
    vt_in_spec = pl.BlockSpec((1, HEADS, HEAD_DIM, SEQ), lambda b, n: (b, 0, 0, 0))
    row_spec = pl.BlockSpec((MOBA_BLOCK, D_MODEL), lambda b, n: (b * nb + n, 0))
    return pl.pallas_call(
        _moba_attn_kernel,
        grid=(BATCH, nb),
        in_specs=[
            pl.BlockSpec((1, HEADS, MOBA_BLOCK, HEAD_DIM), lambda b, n: (b, 0, n, 0)),
            k_spec, vt_in_spec, row_spec, row_spec,
            _const_spec((D_MODEL, D_MODEL)),
            _const_spec((1, D_MODEL)),
            _const_spec((1, D_MODEL)),
        ],
        out_specs=pl.BlockSpec((MOBA_BLOCK, D_MODEL), lambda b, n: (n, b)),
        out_shape=jax.ShapeDtypeStruct((SEQ, BATCH * D_MODEL), F32),
        scratch_shapes=[
            pltpu.VMEM((HEADS, nb, HEAD_DIM), F32),
            pltpu.VMEM((HEADS, MOBA_BLOCK, HEAD_DIM), F32),
        ],
        compiler_params=pltpu.CompilerParams(
            dimension_semantics=("arbitrary", "arbitrary"), vmem_limit_bytes=VMEM_LIMIT_BYTES),
        name="moba_attn",
    )(q, k, vt, gate, x2d, w_out.astype(BF16), ln_g.reshape(1, D_MODEL), ln_b.reshape(1, D_MODEL))


def _s5_discretize_kernel(ar_ref, ai_ref, ldt_ref, br_ref, bi_ref, abr_ref, abi_ref, bbr_ref, bbi_ref):
    ar, ai = ar_ref[...], ai_ref[...]
    dt = jnp.exp(ldt_ref[...])
    mag = jnp.exp(dt * ar)
    abr = mag * jnp.cos(dt * ai)
    abi = mag * jnp.sin(dt * ai)
    nr, ni = abr - 1.0, abi
    den = ar * ar + ai * ai
    zr = (nr * ar + ni * ai) / den
    zi = (ni * ar - nr * ai) / den
    br, bi = br_ref[...], bi_ref[...]
    abr_ref[...] = abr
    abi_ref[...] = abi
    bbr_ref[...] = zr * br - zi * bi
    bbi_ref[...] = zr * bi + zi * br


def _s5_kernel(x_ref, w_in_ref, bd_ref, lam_ref, cd_ref, d_ref, w_glu_ref, b_glu_ref, w_out_ref,
               lng_ref, lnb_ref, out_ref, hst_ref, xs_ref, hb_ref, y_ref):
    ns = S5_SLAB_STATES

    @pl.when(pl.program_id(0) == 0)
    def _():
        hst_ref[...] = jnp.zeros_like(hst_ref)

    x = x_ref[...]
    z = _dot(x.astype(BF16), w_in_ref[...])
    u = z[:, 0:D_MODEL]
    gate = z[:, D_MODEL:2 * D_MODEL]
    u_bf = u.astype(BF16)

    for s in range(S5_NSLAB):
        xs_ref[...] = _dot(u_bf[:, s * S5_SLAB:(s + 1) * S5_SLAB], bd_ref[s])
        lr = jnp.broadcast_to(lam_ref[s, 0:1, :], (BATCH, ns))
        li = jnp.broadcast_to(lam_ref[s, 1:2, :], (BATCH, ns))

        def step(t, h):
            hr, hi = h
            r = pl.multiple_of(t * BATCH, BATCH)
            xr = xs_ref[pl.ds(r, BATCH), 0:ns]
            xi = xs_ref[pl.ds(r, BATCH), ns:2 * ns]
            nhr = lr * hr - li * hi + xr
            nhi = lr * hi + li * hr + xi
            hb_ref[pl.ds(r, BATCH), 0:ns] = nhr.astype(BF16)
            hb_ref[pl.ds(r, BATCH), ns:2 * ns] = nhi.astype(BF16)
            return nhr, nhi

        hr, hi = lax.fori_loop(0, S5_STEPS, step, (hst_ref[s, 0], hst_ref[s, 1]), unroll=True)
        hst_ref[s, 0] = hr
        hst_ref[s, 1] = hi
        y_ref[:, s * S5_SLAB:(s + 1) * S5_SLAB] = _dot(hb_ref[...], cd_ref[s])

    y = y_ref[...] + d_ref[...] * u
    y = jax.nn.gelu(y)
    y = y * jax.nn.sigmoid(_dot(y.astype(BF16), w_glu_ref[...]) + b_glu_ref[...])
    y = y * jax.nn.silu(gate)
    out = _dot(y.astype(BF16), w_out_ref[...])
    out_ref[...] = _residual_layer_norm(x, out, lng_ref[...], lnb_ref[...])


def _s5_layer(x_tb, w_in, a_re, a_im, log_dt, b_re, b_im, c_re, c_im, d_skip, w_glu, b_glu, w_out, ln_g, ln_b):
    gh = S5_GROUPS * S5_GROUP
    rep = lambda a: jnp.repeat(a, S5_GROUP, axis=0)
    small = jax.ShapeDtypeStruct((gh, S5_STATE), F32)
    abr, abi, bbr, bbi = pl.pallas_call(
        _s5_discretize_kernel,
        out_shape=[small] * 4,
        name="s5_discretize",
    )(rep(a_re), rep(a_im), rep(jnp.broadcast_to(log_dt[:, None], (S5_GROUPS, S5_STATE))),
      b_re.transpose(0, 2, 1).reshape(gh, S5_STATE), b_im.transpose(0, 2, 1).reshape(gh, S5_STATE))

    gl = S5_SLAB // S5_GROUP
    eye = jnp.eye(gl, dtype=F32)

    def block_diag_in(bb):
        t = bb.reshape(S5_NSLAB, gl, S5_GROUP, S5_STATE)
        return jnp.einsum('sghp,gk->sghkp', t, eye).reshape(S5_NSLAB, S5_SLAB, gl * S5_STATE)

    def block_diag_out(cc):
        t = cc.reshape(S5_NSLAB, gl, S5_GROUP, S5_STATE)
        return jnp.einsum('sghp,gk->skpgh', t, eye).reshape(S5_NSLAB, gl * S5_STATE, S5_SLAB)

    bd = jnp.concatenate([block_diag_in(bbr), block_diag_in(bbi)], axis=2).astype(BF16)
    cd = jnp.concatenate([block_diag_out(c_re), -block_diag_out(c_im)], axis=1).astype(BF16)
    lam = jnp.stack([abr[::S5_GROUP].reshape(S5_NSLAB, gl * S5_STATE),
                     abi[::S5_GROUP].reshape(S5_NSLAB, gl * S5_STATE)], axis=1)

    rows = S5_STEPS * BATCH
    ns2 = 2 * S5_SLAB_STATES
    row_spec = pl.BlockSpec((rows, D_MODEL), lambda i: (i, 0))
    return pl.pallas_call(
        _s5_kernel,
        grid=(SEQ // S5_STEPS,),
        in_specs=[
            row_spec,
            _const_spec((D_MODEL, 2 * D_MODEL)),
            _const_spec((S5_NSLAB, S5_SLAB, ns2)),
            _const_spec((S5_NSLAB, 2, S5_SLAB_STATES)),
            _const_spec((S5_NSLAB, ns2, S5_SLAB)),
            _const_spec((1, D_MODEL)),
            _const_spec((D_MODEL, D_MODEL)),
            _const_spec((1, D_MODEL)),
            _const_spec((D_MODEL, D_MODEL)),
            _const_spec((1, D_MODEL)),
            _const_spec((1, D_MODEL)),
        ],
        out_specs=row_spec,
        out_shape=jax.ShapeDtypeStruct((SEQ * BATCH, D_MODEL), F32),
        scratch_shapes=[
            pltpu.VMEM((S5_NSLAB, 2, BATCH, S5_SLAB_STATES), F32),
            pltpu.VMEM((rows, ns2), F32),
            pltpu.VMEM((rows, ns2), BF16),
            pltpu.VMEM((rows, D_MODEL), F32),
        ],
        compiler_params=pltpu.CompilerParams(
            dimension_semantics=("arbitrary",), vmem_limit_bytes=VMEM_LIMIT_BYTES),
        name="s5_layer",
    )(x_tb, w_in.astype(BF16), bd, lam, cd, d_skip.reshape(1, D_MODEL), w_glu.astype(BF16),
      b_glu.reshape(1, D_MODEL), w_out.astype(BF16), ln_g.reshape(1, D_MODEL), ln_b.reshape(1, D_MODEL))


def kernel(x, hgrn_lower_bounds,
           l0_w_in, l0_norm_g, l0_w_out, l0_ln_g, l0_ln_b,
           l1_w_in, l1_w_out, l1_ln_g, l1_ln_b,
           l2_w_in, l2_a_re, l2_a_im, l2_log_dt, l2_b_re, l2_b_im, l2_c_re, l2_c_im,
           l2_d, l2_w_glu, l2_b_glu, l2_w_out, l2_ln_g, l2_ln_b,
           l3_w_in, l3_norm_g, l3_w_out, l3_ln_g, l3_ln_b):
    h = x.reshape(BATCH * SEQ, D_MODEL)
    h = _hgrn_layer(h, hgrn_lower_bounds, l0_w_in, l0_norm_g, l0_w_out, l0_ln_g, l0_ln_b,
                    layer=0, in_tb=False, out_tb=False)
    h = _moba_layer(h, l1_w_in, l1_w_out, l1_ln_g, l1_ln_b)
    h = _s5_layer(h.reshape(SEQ * BATCH, D_MODEL), l2_w_in, l2_a_re, l2_a_im, l2_log_dt, l2_b_re, l2_b_im,
                  l2_c_re, l2_c_im, l2_d, l2_w_glu, l2_b_glu, l2_w_out, l2_ln_g, l2_ln_b)
    h = _hgrn_layer(h.reshape(SEQ, BATCH * D_MODEL), hgrn_lower_bounds, l3_w_in, l3_norm_g, l3_w_out,
                    l3_ln_g, l3_ln_b, layer=3, in_tb=True, out_tb=False)
    return h.reshape(BATCH, SEQ, D_MODEL)
```

```python
import functools
import math

import numpy as np
import jax
import jax.numpy as jnp
from jax import lax
from jax.experimental import pallas as pl
from jax.experimental.pallas import tpu as pltpu

D_MODEL = 1024
BATCH = 16
SEQ = 2048
DEPTH = 4
HEADS = 8
HEAD_DIM = 128
MOBA_BLOCK = 256
MOBA_TOPK = 3
ROPE_THETA = 10000.0
S5_GROUP = 16
S5_GROUPS = 64
S5_STATE = 64
ALPHA = (2 * DEPTH) ** 0.25
LN_EPS = 1e-5
RMS_EPS = 1e-6
NEG = -1e30

VMEM_LIMIT_BYTES = 56 * 1024 * 1024
LANES = 128

HGRN_TILE = 256
HGRN_CHUNK = 128
PROJ_TILE = 512
S5_STEPS = 32
S5_SLAB = 256
S5_NSLAB = D_MODEL // S5_SLAB
S5_SLAB_STATES = (S5_SLAB // S5_GROUP) * S5_STATE

F32 = jnp.float32
BF16 = jnp.bfloat16


def _dot(a, b):
    return jnp.dot(a, b, preferred_element_type=F32)


def _dot_nt(a, b):
    return lax.dot_general(a, b, (((1,), (1,)), ((), ())), preferred_element_type=F32)


def _split3(a):
    hi = a.astype(BF16)
    r1 = a - hi.astype(F32)
    mid = r1.astype(BF16)
    lo = (r1 - mid.astype(F32)).astype(BF16)
    return hi, mid, lo


def _residual_layer_norm(x, y, g, b):
    r = ALPHA * x + y
    mu = jnp.mean(r, axis=-1, keepdims=True)
    c = r - mu
    var = jnp.mean(c * c, axis=-1, keepdims=True)
    return c * lax.rsqrt(var + LN_EPS) * g + b


def _const_spec(shape):
    nd = len(shape)
    return pl.BlockSpec(shape, lambda *_: (0,) * nd, pipeline_mode=pl.Buffered(1))


def _hgrn_level_table(c):
    t = np.arange(c)[:, None]
    s = np.arange(c)[None, :]
    x = np.bitwise_xor(t, s)
    lv = np.zeros((c, c), np.int32)
    nz = x > 0
    lv[nz] = 2 ** (np.floor(np.log2(x[nz])).astype(np.int32) + 1)
    lv[s >= t] = 0
    return lv


def _segment_mid_rows(bc, seg):
    c, w = bc.shape
    half = seg // 2
    row = lax.broadcasted_iota(jnp.int32, (8, w), 0)
    pieces = []
    for a in range(0, c, 8):
        cand = [jnp.broadcast_to(bc[a + o + half - 1:a + o + half, :], (8, w)) for o in range(0, 8, seg)]
        out = cand[-1]
        for idx in range(len(cand) - 2, -1, -1):
            out = jnp.where(row < (idx + 1) * seg, cand[idx], out)
        pieces.append(out)
    return jnp.concatenate(pieces, axis=0)


def _level_operand(q, k, bc, seg, second):
    c, _ = bc.shape
    half = seg // 2
    if half >= 8:
        pieces = []
        for a in range(0, c, seg):
            m = bc[a + half - 1:a + half, :]
            pieces.append(k[a:a + half] * jnp.exp2(m - bc[a:a + half]))
            pieces.append(q[a + half:a + seg] * jnp.exp2(bc[a + half:a + seg] - m))
        return jnp.concatenate(pieces, axis=0)
    decay = jnp.exp2(-jnp.abs(bc - _segment_mid_rows(bc, seg)))
    return jnp.where(second, q, k) * decay


HGRN_PROJ_PIECE = 256


def _hgrn_kernel(layer, x_ref, xn_ref, lbl_ref, lv_ref, tri_ref, w_in_ref, ng_ref, w_out_ref, lng_ref, lnb_ref,
                 out_ref, st_ref, o_ref, xb_ref, za_ref, zb_ref):
    t = HGRN_TILE

    def proj_pieces(src, slot, z_ref):
        def cast():
            xb_ref[slot] = src[...].astype(BF16)

        def piece(j):
            cols = slice(j * HGRN_PROJ_PIECE, (j + 1) * HGRN_PROJ_PIECE)
            z_ref[:, cols] = _dot(xb_ref[slot], w_in_ref[:, cols])

        return [cast] + [functools.partial(piece, j) for j in range(4 * D_MODEL // HGRN_PROJ_PIECE)]

    @pl.when((pl.program_id(0) == 0) & (pl.program_id(1) == 0))
    def _():
        for job in proj_pieces(x_ref.at[0:t, :], 0, za_ref):
            job()

    @pl.when(pl.program_id(1) == 0)
    def _():
        st_ref[...] = jnp.zeros_like(st_ref)

    lbl = lbl_ref[...]
    e = jnp.exp(lbl - jnp.max(lbl, axis=0, keepdims=True))
    sm = e / jnp.sum(e, axis=0, keepdims=True)
    lb = jnp.zeros((1, D_MODEL), F32)
    for r in range(1, layer + 1):
        lb = lb + sm[r:r + 1, :]

    mix = functools.partial(_hgrn_mixer, lb, lv_ref, tri_ref, ng_ref, w_out_ref, lng_ref, lnb_ref, st_ref)
    out_ref[0:t, :] = mix(za_ref, x_ref.at[0:t, :], o_ref.at[0], proj_pieces(x_ref.at[t:2 * t, :], 0, zb_ref))
    out_ref[t:2 * t, :] = mix(zb_ref, x_ref.at[t:2 * t, :], o_ref.at[1], proj_pieces(xn_ref, 1, za_ref))


def _hgrn_mixer(lb, lv_ref, tri_ref, ng_ref, w_out_ref, lng_ref, lnb_ref, st_ref, z_ref, x_ref, o_ref, side_jobs):
    c = HGRN_CHUNK
    side_jobs = list(side_jobs)

    def run_side_job():
        if side_jobs:
            side_jobs.pop(0)()

    lv = lv_ref[...]
    tri = tri_ref[...]
    ng = ng_ref[...]
    row_in_chunk = lax.broadcasted_iota(jnp.int32, (c, D_MODEL), 0)

    for ci in range(HGRN_TILE // c):
        rows = slice(ci * c, (ci + 1) * c)
        q = jax.nn.silu(z_ref[rows, 0:D_MODEL])
        f = lb + (1.0 - lb) * jax.nn.sigmoid(z_ref[rows, D_MODEL:2 * D_MODEL])
        g = jnp.log2(f)
        k = 1.0 - f
        v = z_ref[rows, 2 * D_MODEL:3 * D_MODEL]
        gate = jax.nn.silu(z_ref[rows, 3 * D_MODEL:4 * D_MODEL])
        run_side_job()
        g_hi, g_mid, g_lo = _split3(g)
        bc = _dot(tri, g_hi) + _dot(tri, g_mid) + _dot(tri, g_lo)
        run_side_job()

        w_levels = []
        seg = c
        while seg > 2:
            second = (row_in_chunk & (seg - 1)) >= (seg // 2)
            w_levels.append((seg, _level_operand(q, k, bc, seg, second).astype(BF16)))
            seg //= 2
        odd = (row_in_chunk & 1) == 1
        w_levels.append((2, jnp.where(odd, q * f, k).astype(BF16)))

        qd = (q * jnp.exp2(bc)).astype(BF16)
        b_last = bc[c - 1:c, :]
        kd = (k * jnp.exp2(b_last - bc)).astype(BF16)
        d_last = jnp.exp2(b_last)
        qk = q * k
        v_bf = v.astype(BF16)
        run_side_job()

        for h in range(HEADS):
            sl = slice(h * HEAD_DIM, (h + 1) * HEAD_DIM)
            scores = jnp.zeros((c, c), F32)
            for seg, w in w_levels:
                wh = w[:, sl]
                scores = jnp.where(lv == seg, _dot_nt(wh, wh), scores)
            st = st_ref[h]
            o = _dot(scores.astype(BF16), v_bf[:, sl])
            o = o + jnp.sum(qk[:, sl], axis=-1, keepdims=True) * v[:, sl]
            o = o + _dot_nt(qd[:, sl], st.astype(BF16))
            st_ref[h] = st * d_last[:, sl] + _dot(v[:, sl].T.astype(BF16), kd[:, sl])
            o = o * lax.rsqrt(jnp.mean(o * o, axis=-1, keepdims=True) + RMS_EPS) * ng[:, sl]
            o_ref[rows, sl] = (o * gate[:, sl]).astype(BF16)
            run_side_job()

    while side_jobs:
        run_side_job()
    y = _dot(o_ref[...], w_out_ref[...])
    return _residual_layer_norm(x_ref[...], y, lng_ref[...], lnb_ref[...])


def _hgrn_layer(x2d, lb_logits, w_in, norm_g, w_out, ln_g, ln_b, *, layer, in_tb, out_tb):
    nt = SEQ // HGRN_TILE
    ns = nt // 2
    pair = (2 * HGRN_TILE, D_MODEL)
    tile = (HGRN_TILE, D_MODEL)
    tb_spec = pl.BlockSpec(pair, lambda b, c: (c, b))
    bt_spec = pl.BlockSpec(pair, lambda b, c: (b * ns + c, 0))

    def next_tile(b, c):
        return jnp.minimum(2 * (b * ns + c) + 2, BATCH * nt - 1)

    tb_next = pl.BlockSpec(tile, lambda b, c: (next_tile(b, c) % nt, next_tile(b, c) // nt))
    bt_next = pl.BlockSpec(tile, lambda b, c: (next_tile(b, c), 0))
    out_shape = (SEQ, BATCH * D_MODEL) if out_tb else (BATCH * SEQ, D_MODEL)
    lv = jnp.asarray(_hgrn_level_table(HGRN_CHUNK))
    tri = jnp.asarray(np.tril(np.ones((HGRN_CHUNK, HGRN_CHUNK), np.float32)), dtype=BF16)
    return pl.pallas_call(
        functools.partial(_hgrn_kernel, layer),
        grid=(BATCH, ns),
        in_specs=[
            tb_spec if in_tb else bt_spec,
            tb_next if in_tb else bt_next,
            _const_spec((DEPTH, D_MODEL)),
            _const_spec((HGRN_CHUNK, HGRN_CHUNK)),
            _const_spec((HGRN_CHUNK, HGRN_CHUNK)),
            _const_spec((D_MODEL, 4 * D_MODEL)),
            _const_spec((1, D_MODEL)),
            _const_spec((D_MODEL, D_MODEL)),
            _const_spec((1, D_MODEL)),
            _const_spec((1, D_MODEL)),
        ],
        out_specs=tb_spec if out_tb else bt_spec,
        out_shape=jax.ShapeDtypeStruct(out_shape, F32),
        scratch_shapes=[
            pltpu.VMEM((HEADS, HEAD_DIM, HEAD_DIM), F32),
            pltpu.VMEM((2, HGRN_TILE, D_MODEL), BF16),
            pltpu.VMEM((2, HGRN_TILE, D_MODEL), BF16),
            pltpu.VMEM((HGRN_TILE, 4 * D_MODEL), F32),
            pltpu.VMEM((HGRN_TILE, 4 * D_MODEL), F32),
        ],
        compiler_params=pltpu.CompilerParams(
            dimension_semantics=("arbitrary", "arbitrary"),
            vmem_limit_bytes=VMEM_LIMIT_BYTES),
        name=f"hgrn2_layer{layer}",
    )(x2d, x2d, lb_logits, lv, tri, w_in.astype(BF16), norm_g.reshape(1, D_MODEL), w_out.astype(BF16),
      ln_g.reshape(1, D_MODEL), ln_b.reshape(1, D_MODEL))


def _moba_proj_kernel(x_ref, w_in_ref, cq_ref, sq_ref, ck_ref, sk_ref, q_ref, k_ref, vt_ref, gate_ref):
    z = _dot(x_ref[...].astype(BF16), w_in_ref[...])
    cq, sq, ck, sk = cq_ref[...], sq_ref[...], ck_ref[...], sk_ref[...]
    for h in range(HEADS):
        qh = z[:, h * HEAD_DIM:(h + 1) * HEAD_DIM]
        kh = z[:, D_MODEL + h * HEAD_DIM:D_MODEL + (h + 1) * HEAD_DIM]
        vh = z[:, 2 * D_MODEL + h * HEAD_DIM:2 * D_MODEL + (h + 1) * HEAD_DIM]
        q_ref[0, h] = (qh * cq + pltpu.roll(qh, HEAD_DIM // 2, 1) * sq).astype(BF16)
        k_ref[0, h] = (kh * ck + pltpu.roll(kh, HEAD_DIM // 2, 1) * sk).astype(BF16)
        vt_ref[0, h] = vh.T.astype(BF16)
    gate_ref[...] = jax.nn.silu(z[:, 3 * D_MODEL:4 * D_MODEL]).astype(BF16)


def _moba_attn_kernel(q_ref, k_ref, vt_ref, gate_ref, x_ref, w_out_ref, lng_ref, lnb_ref,
                      out_ref, km_ref, o_ref, s_ref):
    n = pl.program_id(1)
    blk = MOBA_BLOCK
    nb = SEQ // blk

    @pl.when(n == 0)
    def _():
        for h in range(HEADS):
            kf = k_ref[0, h].astype(F32).reshape(nb, blk, HEAD_DIM)
            km_ref[h] = jnp.mean(kf, axis=1)

    key = lax.broadcasted_iota(jnp.int32, (blk, blk), 0)
    qry = lax.broadcasted_iota(jnp.int32, (blk, blk), 1)
    causal = key <= qry

    def attend(n_past):
        blocks = [slice(j * blk, (j + 1) * blk) for j in range(n_past + 1)]

        def selection(h, q):
            if n_past <= MOBA_TOPK:
                return None
            km = km_ref[h]
            km_hi = km.astype(BF16)
            km_lo = (km - km_hi.astype(F32)).astype(BF16)
            gsc = _dot_nt(km_hi, q) + _dot_nt(km_lo, q)
            g = [gsc[j:j + 1, :] for j in range(n_past)]
            sel = []
            for j in range(n_past):
                cnt = jnp.zeros((1, blk), F32)
                for i in range(n_past):
                    if i != j:
                        beats = (g[i] >= g[j]) if i < j else (g[i] > g[j])
                        cnt = cnt + beats.astype(F32)
                sel.append(cnt < MOBA_TOPK)
            return sel

        def score_block(h, q, sel, j, slot):
            s = _dot_nt(k_ref[0, h, blocks[j], :], q)
            if j == n_past:
                s = jnp.where(causal, s, NEG)
            elif sel is not None:
                s = jnp.where(sel[j], s, NEG)
            s_ref[slot, blocks[j], :] = s
            return jnp.max(s, axis=0, keepdims=True)

        def value_block(h, m, j, slot):
            p = jnp.exp2(s_ref[slot, blocks[j], :] - m)
            return _dot(vt_ref[0, h, :, blocks[j]], p.astype(BF16)), jnp.sum(p, axis=0, keepdims=True)

        def stage(h_cur, slot_cur, m_cur, h_next, slot_next):
            q = q_ref[0, h_next]
            sel = selection(h_next, q)
            m_next = acc = l = None
            for j in range(n_past + 1):
                mj = score_block(h_next, q, sel, j, slot_next)
                m_next = mj if m_next is None else jnp.maximum(m_next, mj)
                aj, lj = value_block(h_cur, m_cur, j, slot_cur)
                acc = aj if acc is None else acc + aj
                l = lj if l is None else l + lj
            o_ref[h_cur] = (acc * (1.0 / l)).T
            return m_next

        q0 = q_ref[0, 0]
        sel0 = selection(0, q0)
        m0 = None
        for j in range(n_past + 1):
            mj = score_block(0, q0, sel0, j, 0)
            m0 = mj if m0 is None else jnp.maximum(m0, mj)

        def pair_body(i, m_a):
            a = 2 * i
            m_b = stage(a, 0, m_a, a + 1, 1)
            return stage(a + 1, 1, m_b, jnp.minimum(a + 2, HEADS - 1), 0)

        lax.fori_loop(0, HEADS // 2, pair_body, m0)

    for n_past in range(nb):
        pl.when(n == n_past)(functools.partial(attend, n_past))

    o = jnp.concatenate([o_ref[h] for h in range(HEADS)], axis=1)
    o = (o * gate_ref[...].astype(F32)).astype(BF16)
    y = _dot(o, w_out_ref[...])
    out_ref[...] = _residual_layer_norm(x_ref[...], y, lng_ref[...], lnb_ref[...])


def _rope_tables():
    pos = np.arange(SEQ, dtype=np.float32)
    inv_freq = (1.0 / (np.float32(ROPE_THETA) ** (np.arange(0, HEAD_DIM, 2, dtype=np.float32) / HEAD_DIM))
                ).astype(np.float32)
    ang = pos[:, None] * inv_freq[None, :]
    cos, sin = np.cos(ang).astype(np.float32), np.sin(ang).astype(np.float32)
    c = np.concatenate([cos, cos], axis=1)
    s = np.concatenate([-sin, sin], axis=1)
    scale = np.float32(HEAD_DIM ** -0.5 * math.log2(math.e))
    return c * scale, s * scale, c, s


def _moba_layer(x2d, w_in, w_out, ln_g, ln_b):
    tiles_per_seq = SEQ // PROJ_TILE
    hshape = (BATCH, HEADS, SEQ, HEAD_DIM)
    hspec = pl.BlockSpec((1, HEADS, PROJ_TILE, HEAD_DIM), lambda i: (i // tiles_per_seq, 0, i % tiles_per_seq, 0))
    tspec = pl.BlockSpec((PROJ_TILE, HEAD_DIM), lambda i: (i % tiles_per_seq, 0))
    nb = SEQ // MOBA_BLOCK
    vt_shape = (BATCH, HEADS, HEAD_DIM, SEQ)
    vt_spec = pl.BlockSpec((1, HEADS, HEAD_DIM, PROJ_TILE),
                           lambda i: (i // tiles_per_seq, 0, 0, i % tiles_per_seq))
    cq, sq, ck, sk = (jnp.asarray(t) for t in _rope_tables())
    q, k, vt, gate = pl.pallas_call(
        _moba_proj_kernel,
        grid=(BATCH * tiles_per_seq,),
        in_specs=[
            pl.BlockSpec((PROJ_TILE, D_MODEL), lambda i: (i, 0)),
            _const_spec((D_MODEL, 4 * D_MODEL)),
            tspec, tspec, tspec, tspec,
        ],
        out_specs=[hspec, hspec, vt_spec, pl.BlockSpec((PROJ_TILE, D_MODEL), lambda i: (i, 0))],
        out_shape=[jax.ShapeDtypeStruct(hshape, BF16)] * 2 + [jax.ShapeDtypeStruct(vt_shape, BF16),
                                                              jax.ShapeDtypeStruct((BATCH * SEQ, D_MODEL), BF16)],
        compiler_params=pltpu.CompilerParams(
            dimension_semantics=("arbitrary",), vmem_limit_bytes=VMEM_LIMIT_BYTES),
        name="moba_proj",
    )(x2d, w_in.astype(BF16), cq, sq, ck, sk)

    k_spec = pl.BlockSpec((1, HEADS, SEQ, HEAD_DIM), lambda b, n: (b, 0, 0, 0))
    vt_in_spec = pl.BlockSpec((1, HEADS, HEAD_DIM, SEQ), lambda b, n: (b, 0, 0, 0))
    row_spec = pl.BlockSpec((MOBA_BLOCK, D_MODEL), lambda b, n: (b * nb + n, 0))
    return pl.pallas_call(
        _moba_attn_kernel,
        grid=(BATCH, nb),
        in_specs=[
            pl.BlockSpec((1, HEADS, MOBA_BLOCK, HEAD_DIM), lambda b, n: (b, 0, n, 0)),
            k_spec, vt_in_spec, row_spec, row_spec,
            _const_spec((D_MODEL, D_MODEL)),
            _const_spec((1, D_MODEL)),
            _const_spec((1, D_MODEL)),
        ],
        out_specs=pl.BlockSpec((MOBA_BLOCK, D_MODEL), lambda b, n: (n, b)),
        out_shape=jax.ShapeDtypeStruct((SEQ, BATCH * D_MODEL), F32),
        scratch_shapes=[
            pltpu.VMEM((HEADS, nb, HEAD_DIM), F32),
            pltpu.VMEM((HEADS, MOBA_BLOCK, HEAD_DIM), F32),
            pltpu.VMEM((2, SEQ, MOBA_BLOCK), F32),
        ],
        compiler_params=pltpu.CompilerParams(
            dimension_semantics=("arbitrary", "arbitrary"), vmem_limit_bytes=VMEM_LIMIT_BYTES),
        name="moba_attn",
    )(q, k, vt, gate, x2d, w_out.astype(BF16), ln_g.reshape(1, D_MODEL), ln_b.reshape(1, D_MODEL))


def _s5_discretize_kernel(ar_ref, ai_ref, ldt_ref, br_ref, bi_ref, abr_ref, abi_ref, bbr_ref, bbi_ref):
    ar, ai = ar_ref[...], ai_ref[...]
    dt = jnp.exp(ldt_ref[...])
    mag = jnp.exp(dt * ar)
    abr = mag * jnp.cos(dt * ai)
    abi = mag * jnp.sin(dt * ai)
    nr, ni = abr - 1.0, abi
    den = ar * ar + ai * ai
    zr = (nr * ar + ni * ai) / den
    zi = (ni * ar - nr * ai) / den
    br, bi = br_ref[...], bi_ref[...]
    abr_ref[...] = abr
    abi_ref[...] = abi
    bbr_ref[...] = zr * br - zi * bi
    bbi_ref[...] = zr * bi + zi * br


def _s5_kernel(x_ref, w_in_ref, bd_ref, lam_ref, cd_ref, d_ref, w_glu_ref, b_glu_ref, w_out_ref,
               lng_ref, lnb_ref, out_ref, hst_ref, xs_ref, hb_ref, y_ref, xt_ref, ot_ref):
    ns = S5_SLAB_STATES
    lane_slabs = D_MODEL // LANES

    @pl.when(pl.program_id(0) == 0)
    def _():
        hst_ref[...] = jnp.zeros_like(hst_ref)

    for b in range(BATCH):
        for ls in range(lane_slabs):
            col = b * D_MODEL + ls * LANES
            xt_ref[ls, pl.ds(b, S5_STEPS, stride=BATCH), :] = x_ref[:, col:col + LANES]
    x = jnp.concatenate([xt_ref[ls] for ls in range(lane_slabs)], axis=1)
    z = _dot(x.astype(BF16), w_in_ref[...])
    u = z[:, 0:D_MODEL]
    gate = z[:, D_MODEL:2 * D_MODEL]
    u_bf = u.astype(BF16)

    for s in range(S5_NSLAB):
        xs_ref[...] = _dot(u_bf[:, s * S5_SLAB:(s + 1) * S5_SLAB], bd_ref[s])
        lr = jnp.broadcast_to(lam_ref[s, 0:1, :], (BATCH, ns))
        li = jnp.broadcast_to(lam_ref[s, 1:2, :], (BATCH, ns))

        def step(t, h):
            hr, hi = h
            r = pl.multiple_of(t * BATCH, BATCH)
            xr = xs_ref[pl.ds(r, BATCH), 0:ns]
            xi = xs_ref[pl.ds(r, BATCH), ns:2 * ns]
            nhr = lr * hr - li * hi + xr
            nhi = lr * hi + li * hr + xi
            hb_ref[pl.ds(r, BATCH), 0:ns] = nhr.astype(BF16)
            hb_ref[pl.ds(r, BATCH), ns:2 * ns] = nhi.astype(BF16)
            return nhr, nhi

        hr, hi = lax.fori_loop(0, S5_STEPS, step, (hst_ref[s, 0], hst_ref[s, 1]), unroll=True)
        hst_ref[s, 0] = hr
        hst_ref[s, 1] = hi
        y_ref[:, s * S5_SLAB:(s + 1) * S5_SLAB] = _dot(hb_ref[...], cd_ref[s])

    y = y_ref[...] + d_ref[...] * u
    y = jax.nn.gelu(y)
    y = y * jax.nn.sigmoid(_dot(y.astype(BF16), w_glu_ref[...]) + b_glu_ref[...])
    y = y * jax.nn.silu(gate)
    out = _dot(y.astype(BF16), w_out_ref[...])
    res = _residual_layer_norm(x, out, lng_ref[...], lnb_ref[...])
    for ls in range(lane_slabs):
        ot_ref[ls] = res[:, ls * LANES:(ls + 1) * LANES]
    for b in range(BATCH):
        for ls in range(lane_slabs):
            col = b * D_MODEL + ls * LANES
            out_ref[:, col:col + LANES] = ot_ref[ls, pl.ds(b, S5_STEPS, stride=BATCH), :]


def _s5_layer(x_sbd, w_in, a_re, a_im, log_dt, b_re, b_im, c_re, c_im, d_skip, w_glu, b_glu, w_out, ln_g, ln_b):
    gh = S5_GROUPS * S5_GROUP
    rep = lambda a: jnp.repeat(a, S5_GROUP, axis=0)
    small = jax.ShapeDtypeStruct((gh, S5_STATE), F32)
    abr, abi, bbr, bbi = pl.pallas_call(
        _s5_discretize_kernel,
        out_shape=[small] * 4,
        name="s5_discretize",
    )(rep(a_re), rep(a_im), rep(jnp.broadcast_to(log_dt[:, None], (S5_GROUPS, S5_STATE))),
      b_re.transpose(0, 2, 1).reshape(gh, S5_STATE), b_im.transpose(0, 2, 1).reshape(gh, S5_STATE))

    gl = S5_SLAB // S5_GROUP
    eye = jnp.eye(gl, dtype=F32)

    def block_diag_in(bb):
        t = bb.reshape(S5_NSLAB, gl, S5_GROUP, S5_STATE)
        return jnp.einsum('sghp,gk->sghkp', t, eye).reshape(S5_NSLAB, S5_SLAB, gl * S5_STATE)

    def block_diag_out(cc):
        t = cc.reshape(S5_NSLAB, gl, S5_GROUP, S5_STATE)
        return jnp.einsum('sghp,gk->skpgh', t, eye).reshape(S5_NSLAB, gl * S5_STATE, S5_SLAB)

    bd = jnp.concatenate([block_diag_in(bbr), block_diag_in(bbi)], axis=2).astype(BF16)
    cd = jnp.concatenate([block_diag_out(c_re), -block_diag_out(c_im)], axis=1).astype(BF16)
    lam = jnp.stack([abr[::S5_GROUP].reshape(S5_NSLAB, gl * S5_STATE),
                     abi[::S5_GROUP].reshape(S5_NSLAB, gl * S5_STATE)], axis=1)

    rows = S5_STEPS * BATCH
    ns2 = 2 * S5_SLAB_STATES
    row_spec = pl.BlockSpec((S5_STEPS, BATCH * D_MODEL), lambda i: (i, 0))
    return pl.pallas_call(
        _s5_kernel,
        grid=(SEQ // S5_STEPS,),
        in_specs=[
            row_spec,
            _const_spec((D_MODEL, 2 * D_MODEL)),
            _const_spec((S5_NSLAB, S5_SLAB, ns2)),
            _const_spec((S5_NSLAB, 2, S5_SLAB_STATES)),
            _const_spec((S5_NSLAB, ns2, S5_SLAB)),
            _const_spec((1, D_MODEL)),
            _const_spec((D_MODEL, D_MODEL)),
            _const_spec((1, D_MODEL)),
            _const_spec((D_MODEL, D_MODEL)),
            _const_spec((1, D_MODEL)),
            _const_spec((1, D_MODEL)),
        ],
        out_specs=row_spec,
        out_shape=jax.ShapeDtypeStruct((SEQ, BATCH * D_MODEL), F32),
        scratch_shapes=[
            pltpu.VMEM((S5_NSLAB, 2, BATCH, S5_SLAB_STATES), F32),
            pltpu.VMEM((rows, ns2), F32),
            pltpu.VMEM((rows, ns2), BF16),
            pltpu.VMEM((rows, D_MODEL), F32),
            pltpu.VMEM((D_MODEL // LANES, rows, LANES), F32),
            pltpu.VMEM((D_MODEL // LANES, rows, LANES), F32),
        ],
        compiler_params=pltpu.CompilerParams(
            dimension_semantics=("arbitrary",), vmem_limit_bytes=VMEM_LIMIT_BYTES),
        name="s5_layer",
    )(x_sbd, w_in.astype(BF16), bd, lam, cd, d_skip.reshape(1, D_MODEL), w_glu.astype(BF16),
      b_glu.reshape(1, D_MODEL), w_out.astype(BF16), ln_g.reshape(1, D_MODEL), ln_b.reshape(1, D_MODEL))


def kernel(x, hgrn_lower_bounds,
           l0_w_in, l0_norm_g, l0_w_out, l0_ln_g, l0_ln_b,
           l1_w_in, l1_w_out, l1_ln_g, l1_ln_b,
           l2_w_in, l2_a_re, l2_a_im, l2_log_dt, l2_b_re, l2_b_im, l2_c_re, l2_c_im,
           l2_d, l2_w_glu, l2_b_glu, l2_w_out, l2_ln_g, l2_ln_b,
           l3_w_in, l3_norm_g, l3_w_out, l3_ln_g, l3_ln_b):
    h = x.reshape(BATCH * SEQ, D_MODEL)
    h = _hgrn_layer(h, hgrn_lower_bounds, l0_w_in, l0_norm_g, l0_w_out, l0_ln_g, l0_ln_b,
                    layer=0, in_tb=False, out_tb=False)
    h = _moba_layer(h, l1_w_in, l1_w_out, l1_ln_g, l1_ln_b)
    h = _s5_layer(h, l2_w_in, l2_a_re, l2_a_im, l2_log_dt, l2_b_re, l2_b_im,
                  l2_c_re, l2_c_im, l2_d, l2_w_glu, l2_b_glu, l2_w_out, l2_ln_g, l2_ln_b)
    h = _hgrn_layer(h, hgrn_lower_bounds, l3_w_in, l3_norm_g, l3_w_out,
                    l3_ln_g, l3_ln_b, layer=3, in_tb=True, out_tb=False)
    return h.reshape(BATCH, SEQ, D_MODEL)
```

```python
import functools
import math

import numpy as np
import jax
import jax.numpy as jnp
from jax import lax
from jax.experimental import pallas as pl
from jax.experimental.pallas import tpu as pltpu

D_MODEL = 1024
BATCH = 16
SEQ = 2048
DEPTH = 4
HEADS = 8
HEAD_DIM = 128
MOBA_BLOCK = 256
MOBA_TOPK = 3
ROPE_THETA = 10000.0
S5_GROUP = 16
S5_GROUPS = 64
S5_STATE = 64
ALPHA = (2 * DEPTH) ** 0.25
LN_EPS = 1e-5
RMS_EPS = 1e-6
NEG = -1e30

VMEM_LIMIT_BYTES = 56 * 1024 * 1024
LANES = 128

HGRN_TILE = 256
HGRN_CHUNK = 128
PROJ_TILE = 512
S5_STEPS = 32
S5_SLAB = 256
S5_NSLAB = D_MODEL // S5_SLAB
S5_SLAB_STATES = (S5_SLAB // S5_GROUP) * S5_STATE

F32 = jnp.float32
BF16 = jnp.bfloat16


def _dot(a, b):
    return jnp.dot(a, b, preferred_element_type=F32)


def _dot_nt(a, b):
    return lax.dot_general(a, b, (((1,), (1,)), ((), ())), preferred_element_type=F32)


def _split2(a):
    hi = a.astype(BF16)
    lo = (a - hi.astype(F32)).astype(BF16)
    return hi, lo


def _block_diag(a, b):
    zero = jnp.zeros_like(a)
    return jnp.concatenate([jnp.concatenate([a, zero], axis=1), jnp.concatenate([zero, b], axis=1)], axis=0)


def _residual_layer_norm(x, y, g, b):
    r = ALPHA * x + y
    mu = jnp.mean(r, axis=-1, keepdims=True)
    c = r - mu
    var = jnp.mean(c * c, axis=-1, keepdims=True)
    return c * lax.rsqrt(var + LN_EPS) * g + b


def _const_spec(shape):
    nd = len(shape)
    return pl.BlockSpec(shape, lambda *_: (0,) * nd, pipeline_mode=pl.Buffered(1))


def _hgrn_level_table(c):
    t = np.arange(c)[:, None]
    s = np.arange(c)[None, :]
    x = np.bitwise_xor(t, s)
    lv = np.zeros((c, c), np.int32)
    nz = x > 0
    lv[nz] = 2 ** (np.floor(np.log2(x[nz])).astype(np.int32) + 1)
    lv[s >= t] = 0
    return lv


def _segment_mid_rows(bc, seg):
    c, w = bc.shape
    half = seg // 2
    row = lax.broadcasted_iota(jnp.int32, (8, w), 0)
    pieces = []
    for a in range(0, c, 8):
        cand = [jnp.broadcast_to(bc[a + o + half - 1:a + o + half, :], (8, w)) for o in range(0, 8, seg)]
        out = cand[-1]
        for idx in range(len(cand) - 2, -1, -1):
            out = jnp.where(row < (idx + 1) * seg, cand[idx], out)
        pieces.append(out)
    return jnp.concatenate(pieces, axis=0)


def _level_operand(q, k, bc, seg, second):
    c, _ = bc.shape
    half = seg // 2
    if half >= 8:
        pieces = []
        for a in range(0, c, seg):
            m = bc[a + half - 1:a + half, :]
            pieces.append(k[a:a + half] * jnp.exp2(m - bc[a:a + half]))
            pieces.append(q[a + half:a + seg] * jnp.exp2(bc[a + half:a + seg] - m))
        return jnp.concatenate(pieces, axis=0)
    decay = jnp.exp2(-jnp.abs(bc - _segment_mid_rows(bc, seg)))
    return jnp.where(second, q, k) * decay


HGRN_PROJ_PIECE = 256


def _hgrn_kernel(layer, x_ref, xn_ref, lbl_ref, lv_ref, tri_ref, w_in_ref, ng_ref, w_out_ref, lng_ref, lnb_ref,
                 out_ref, st_ref, o_ref, xb_ref, za_ref, zb_ref):
    t = HGRN_TILE

    def proj_pieces(src, slot, z_ref):
        def cast():
            xb_ref[slot] = src[...].astype(BF16)

        def piece(j):
            cols = slice(j * HGRN_PROJ_PIECE, (j + 1) * HGRN_PROJ_PIECE)
            z_ref[:, cols] = _dot(xb_ref[slot], w_in_ref[:, cols])

        return [cast] + [functools.partial(piece, j) for j in range(4 * D_MODEL // HGRN_PROJ_PIECE)]

    @pl.when((pl.program_id(0) == 0) & (pl.program_id(1) == 0))
    def _():
        for job in proj_pieces(x_ref.at[0:t, :], 0, za_ref):
            job()

    @pl.when(pl.program_id(1) == 0)
    def _():
        st_ref[...] = jnp.zeros_like(st_ref)

    lbl = lbl_ref[...]
    e = jnp.exp(lbl - jnp.max(lbl, axis=0, keepdims=True))
    sm = e / jnp.sum(e, axis=0, keepdims=True)
    lb = jnp.zeros((1, D_MODEL), F32)
    for r in range(1, layer + 1):
        lb = lb + sm[r:r + 1, :]

    mix = functools.partial(_hgrn_mixer, lb, lv_ref, tri_ref, ng_ref, w_out_ref, lng_ref, lnb_ref, st_ref)
    out_ref[0:t, :] = mix(za_ref, x_ref.at[0:t, :], o_ref.at[0], proj_pieces(x_ref.at[t:2 * t, :], 0, zb_ref))
    out_ref[t:2 * t, :] = mix(zb_ref, x_ref.at[t:2 * t, :], o_ref.at[1], proj_pieces(xn_ref, 1, za_ref))


def _hgrn_mixer(lb, lv_ref, tri_ref, ng_ref, w_out_ref, lng_ref, lnb_ref, st_ref, z_ref, x_ref, o_ref, side_jobs):
    c = HGRN_CHUNK
    side_jobs = list(side_jobs)

    def run_side_job():
        if side_jobs:
            side_jobs.pop(0)()

    lv2 = lv_ref[...]
    tri2 = tri_ref[...]
    ng = ng_ref[...]
    row_in_chunk = lax.broadcasted_iota(jnp.int32, (c, D_MODEL), 0)

    for ci in range(HGRN_TILE // c):
        rows = slice(ci * c, (ci + 1) * c)
        q = jax.nn.silu(z_ref[rows, 0:D_MODEL])
        f = lb + (1.0 - lb) * jax.nn.sigmoid(z_ref[rows, D_MODEL:2 * D_MODEL])
        g = jnp.log2(f)
        k = 1.0 - f
        v = z_ref[rows, 2 * D_MODEL:3 * D_MODEL]
        gate = jax.nn.silu(z_ref[rows, 3 * D_MODEL:4 * D_MODEL])
        run_side_job()
        bc = _dot(tri2, jnp.concatenate(_split2(g), axis=0))
        run_side_job()

        w_levels = []
        seg = c
        while seg > 2:
            second = (row_in_chunk & (seg - 1)) >= (seg // 2)
            w_levels.append((seg, _level_operand(q, k, bc, seg, second).astype(BF16)))
            seg //= 2
        odd = (row_in_chunk & 1) == 1
        w_levels.append((2, jnp.where(odd, q * f, k).astype(BF16)))

        qd = (q * jnp.exp2(bc)).astype(BF16)
        b_last = bc[c - 1:c, :]
        kd = (k * jnp.exp2(b_last - bc)).astype(BF16)
        d_last = jnp.exp2(b_last)
        qk = q * k
        run_side_job()

        for hp in range(HEADS // 2):
            pair = slice(2 * hp * HEAD_DIM, 2 * (hp + 1) * HEAD_DIM)
            scores2 = jnp.zeros((c, 2 * c), F32)
            for seg, w in w_levels:
                w2 = w[:, pair]
                scores2 = jnp.where(lv2 == seg, _dot_nt(w2, _block_diag(w2[:, :HEAD_DIM], w2[:, HEAD_DIM:])), scores2)
            st2 = st_ref[:, pair]
            vt = [v[:, pair][:, i * HEAD_DIM:(i + 1) * HEAD_DIM].T.astype(BF16) for i in range(2)]
            kd2 = kd[:, pair]
            upd = _dot(jnp.concatenate(vt, axis=1), _block_diag(kd2[:, :HEAD_DIM], kd2[:, HEAD_DIM:]))
            st_ref[:, pair] = st2 * d_last[:, pair] + upd
            for i in range(2):
                sl = slice((2 * hp + i) * HEAD_DIM, (2 * hp + i + 1) * HEAD_DIM)
                lhs = jnp.concatenate([scores2[:, i * c:(i + 1) * c].astype(BF16), qd[:, sl]], axis=1)
                rhs_t = jnp.concatenate([vt[i], st2[:, i * HEAD_DIM:(i + 1) * HEAD_DIM].astype(BF16)], axis=1)
                o = _dot_nt(lhs, rhs_t) + jnp.sum(qk[:, sl], axis=-1, keepdims=True) * v[:, sl]
                o = o * lax.rsqrt(jnp.mean(o * o, axis=-1, keepdims=True) + RMS_EPS) * ng[:, sl]
                o_ref[rows, sl] = (o * gate[:, sl]).astype(BF16)
                run_side_job()

    while side_jobs:
        run_side_job()
    y = _dot(o_ref[...], w_out_ref[...])
    return _residual_layer_norm(x_ref[...], y, lng_ref[...], lnb_ref[...])


def _hgrn_layer(x2d, lb_logits, w_in, norm_g, w_out, ln_g, ln_b, *, layer, in_tb, out_tb):
    nt = SEQ // HGRN_TILE
    ns = nt // 2
    pair = (2 * HGRN_TILE, D_MODEL)
    tile = (HGRN_TILE, D_MODEL)
    tb_spec = pl.BlockSpec(pair, lambda b, c: (c, b))
    bt_spec = pl.BlockSpec(pair, lambda b, c: (b * ns + c, 0))

    def next_tile(b, c):
        return jnp.minimum(2 * (b * ns + c) + 2, BATCH * nt - 1)

    tb_next = pl.BlockSpec(tile, lambda b, c: (next_tile(b, c) % nt, next_tile(b, c) // nt))
    bt_next = pl.BlockSpec(tile, lambda b, c: (next_tile(b, c), 0))
    out_shape = (SEQ, BATCH * D_MODEL) if out_tb else (BATCH * SEQ, D_MODEL)
    lv = _hgrn_level_table(HGRN_CHUNK)
    lv2 = jnp.asarray(np.concatenate([lv, lv], axis=1))
    tri = np.tril(np.ones((HGRN_CHUNK, HGRN_CHUNK), np.float32))
    tri2 = jnp.asarray(np.concatenate([tri, tri], axis=1), dtype=BF16)
    return pl.pallas_call(
        functools.partial(_hgrn_kernel, layer),
        grid=(BATCH, ns),
        in_specs=[
            tb_spec if in_tb else bt_spec,
            tb_next if in_tb else bt_next,
            _const_spec((DEPTH, D_MODEL)),
            _const_spec((HGRN_CHUNK, 2 * HGRN_CHUNK)),
            _const_spec((HGRN_CHUNK, 2 * HGRN_CHUNK)),
            _const_spec((D_MODEL, 4 * D_MODEL)),
            _const_spec((1, D_MODEL)),
            _const_spec((D_MODEL, D_MODEL)),
            _const_spec((1, D_MODEL)),
            _const_spec((1, D_MODEL)),
        ],
        out_specs=tb_spec if out_tb else bt_spec,
        out_shape=jax.ShapeDtypeStruct(out_shape, F32),
        scratch_shapes=[
            pltpu.VMEM((HEAD_DIM, D_MODEL), F32),
            pltpu.VMEM((2, HGRN_TILE, D_MODEL), BF16),
            pltpu.VMEM((2, HGRN_TILE, D_MODEL), BF16),
            pltpu.VMEM((HGRN_TILE, 4 * D_MODEL), F32),
            pltpu.VMEM((HGRN_TILE, 4 * D_MODEL), F32),
        ],
        compiler_params=pltpu.CompilerParams(
            dimension_semantics=("arbitrary", "arbitrary"),
            vmem_limit_bytes=VMEM_LIMIT_BYTES),
        name=f"hgrn2_layer{layer}",
    )(x2d, x2d, lb_logits, lv2, tri2, w_in.astype(BF16), norm_g.reshape(1, D_MODEL), w_out.astype(BF16),
      ln_g.reshape(1, D_MODEL), ln_b.reshape(1, D_MODEL))


def _moba_proj_kernel(x_ref, w_in_ref, cq_ref, sq_ref, ck_ref, sk_ref, q_ref, k_ref, vt_ref, gate_ref):
    z = _dot(x_ref[...].astype(BF16), w_in_ref[...])
    cq, sq, ck, sk = cq_ref[...], sq_ref[...], ck_ref[...], sk_ref[...]
    for h in range(HEADS):
        qh = z[:, h * HEAD_DIM:(h + 1) * HEAD_DIM]
        kh = z[:, D_MODEL + h * HEAD_DIM:D_MODEL + (h + 1) * HEAD_DIM]
        vh = z[:, 2 * D_MODEL + h * HEAD_DIM:2 * D_MODEL + (h + 1) * HEAD_DIM]
        q_ref[0, h] = (qh * cq + pltpu.roll(qh, HEAD_DIM // 2, 1) * sq).astype(BF16)
        k_ref[0, h] = (kh * ck + pltpu.roll(kh, HEAD_DIM // 2, 1) * sk).astype(BF16)
        vt_ref[0, h] = vh.T.astype(BF16)
    gate_ref[...] = jax.nn.silu(z[:, 3 * D_MODEL:4 * D_MODEL]).astype(BF16)


def _moba_attn_kernel(q_ref, k_ref, vt_ref, gate_ref, x_ref, w_out_ref, lng_ref, lnb_ref,
                      out_ref, km_ref, o_ref, s_ref):
    n = pl.program_id(1)
    blk = MOBA_BLOCK
    nb = SEQ // blk

    @pl.when(n == 0)
    def _():
        for h in range(HEADS):
            kf = k_ref[0, h].astype(F32).reshape(nb, blk, HEAD_DIM)
            km_ref[h] = jnp.mean(kf, axis=1)

    key = lax.broadcasted_iota(jnp.int32, (blk, blk), 0)
    qry = lax.broadcasted_iota(jnp.int32, (blk, blk), 1)
    causal = key <= qry

    def attend(n_past):
        blocks = [slice(j * blk, (j + 1) * blk) for j in range(n_past + 1)]

        def selection(h, q):
            if n_past <= MOBA_TOPK:
                return None
            km = km_ref[h]
            km_hi = km.astype(BF16)
            km_lo = (km - km_hi.astype(F32)).astype(BF16)
            gsc = _dot_nt(km_hi, q) + _dot_nt(km_lo, q)
            g = [gsc[j:j + 1, :] for j in range(n_past)]
            sel = []
            for j in range(n_past):
                cnt = jnp.zeros((1, blk), F32)
                for i in range(n_past):
                    if i != j:
                        beats = (g[i] >= g[j]) if i < j else (g[i] > g[j])
                        cnt = cnt + beats.astype(F32)
                sel.append(cnt < MOBA_TOPK)
            return sel

        def score_block(h, q, sel, j, slot):
            s = _dot_nt(k_ref[0, h, blocks[j], :], q)
            if j == n_past:
                s = jnp.where(causal, s, NEG)
            elif sel is not None:
                s = jnp.where(sel[j], s, NEG)
            s_ref[slot, blocks[j], :] = s
            return jnp.max(s, axis=0, keepdims=True)

        def value_block(h, m, j, slot):
            p = jnp.exp2(s_ref[slot, blocks[j], :] - m)
            return _dot(vt_ref[0, h, :, blocks[j]], p.astype(BF16)), jnp.sum(p, axis=0, keepdims=True)

        def stage(h_cur, slot_cur, m_cur, h_next, slot_next):
            q = q_ref[0, h_next]
            sel = selection(h_next, q)
            m_next = acc = l = None
            for j in range(n_past + 1):
                mj = score_block(h_next, q, sel, j, slot_next)
                m_next = mj if m_next is None else jnp.maximum(m_next, mj)
                aj, lj = value_block(h_cur, m_cur, j, slot_cur)
                acc = aj if acc is None else acc + aj
                l = lj if l is None else l + lj
            o_ref[h_cur] = (acc * (1.0 / l)).T
            return m_next

        q0 = q_ref[0, 0]
        sel0 = selection(0, q0)
        m0 = None
        for j in range(n_past + 1):
            mj = score_block(0, q0, sel0, j, 0)
            m0 = mj if m0 is None else jnp.maximum(m0, mj)

        def pair_body(i, m_a):
            a = 2 * i
            m_b = stage(a, 0, m_a, a + 1, 1)
            return stage(a + 1, 1, m_b, jnp.minimum(a + 2, HEADS - 1), 0)

        lax.fori_loop(0, HEADS // 2, pair_body, m0)

    for n_past in range(nb):
        pl.when(n == n_past)(functools.partial(attend, n_past))

    o = jnp.concatenate([o_ref[h] for h in range(HEADS)], axis=1)
    o = (o * gate_ref[...].astype(F32)).astype(BF16)
    y = _dot(o, w_out_ref[...])
    out_ref[...] = _residual_layer_norm(x_ref[...], y, lng_ref[...], lnb_ref[...])


def _rope_tables():
    pos = np.arange(SEQ, dtype=np.float32)
    inv_freq = (1.0 / (np.float32(ROPE_THETA) ** (np.arange(0, HEAD_DIM, 2, dtype=np.float32) / HEAD_DIM))
                ).astype(np.float32)
    ang = pos[:, None] * inv_freq[None, :]
    cos, sin = np.cos(ang).astype(np.float32), np.sin(ang).astype(np.float32)
    c = np.concatenate([cos, cos], axis=1)
    s = np.concatenate([-sin, sin], axis=1)
    scale = np.float32(HEAD_DIM ** -0.5 * math.log2(math.e))
    return c * scale, s * scale, c, s


def _moba_layer(x2d, w_in, w_out, ln_g, ln_b):
    tiles_per_seq = SEQ // PROJ_TILE
    hshape = (BATCH, HEADS, SEQ, HEAD_DIM)
    hspec = pl.BlockSpec((1, HEADS, PROJ_TILE, HEAD_DIM), lambda i: (i // tiles_per_seq, 0, i % tiles_per_seq, 0))
    tspec = pl.BlockSpec((PROJ_TILE, HEAD_DIM), lambda i: (i % tiles_per_seq, 0))
    nb = SEQ // MOBA_BLOCK
    vt_shape = (BATCH, HEADS, HEAD_DIM, SEQ)
    vt_spec = pl.BlockSpec((1, HEADS, HEAD_DIM, PROJ_TILE),
                           lambda i: (i // tiles_per_seq, 0, 0, i % tiles_per_seq))
    cq, sq, ck, sk = (jnp.asarray(t) for t in _rope_tables())
    q, k, vt, gate = pl.pallas_call(
        _moba_proj_kernel,
        grid=(BATCH * tiles_per_seq,),
        in_specs=[
            pl.BlockSpec((PROJ_TILE, D_MODEL), lambda i: (i, 0)),
            _const_spec((D_MODEL, 4 * D_MODEL)),
            tspec, tspec, tspec, tspec,
        ],
        out_specs=[hspec, hspec, vt_spec, pl.BlockSpec((PROJ_TILE, D_MODEL), lambda i: (i, 0))],
        out_shape=[jax.ShapeDtypeStruct(hshape, BF16)] * 2 + [jax.ShapeDtypeStruct(vt_shape, BF16),
                                                              jax.ShapeDtypeStruct((BATCH * SEQ, D_MODEL), BF16)],
        compiler_params=pltpu.CompilerParams(
            dimension_semantics=("arbitrary",), vmem_limit_bytes=VMEM_LIMIT_BYTES),
        name="moba_proj",
    )(x2d, w_in.astype(BF16), cq, sq, ck, sk)

    k_spec = pl.BlockSpec((1, HEADS, SEQ, HEAD_DIM), lambda b, n: (b, 0, 0, 0))
    vt_in_spec = pl.BlockSpec((1, HEADS, HEAD_DIM, SEQ), lambda b, n: (b, 0, 0, 0))
    row_spec = pl.BlockSpec((MOBA_BLOCK, D_MODEL), lambda b, n: (b * nb + n, 0))
    return pl.pallas_call(
        _moba_attn_kernel,
        grid=(BATCH, nb),
        in_specs=[
            pl.BlockSpec((1, HEADS, MOBA_BLOCK, HEAD_DIM), lambda b, n: (b, 0, n, 0)),
            k_spec, vt_in_spec, row_spec, row_spec,
            _const_spec((D_MODEL, D_MODEL)),
            _const_spec((1, D_MODEL)),
            _const_spec((1, D_MODEL)),
        ],
        out_specs=pl.BlockSpec((MOBA_BLOCK, D_MODEL), lambda b, n: (n, b)),
        out_shape=jax.ShapeDtypeStruct((SEQ, BATCH * D_MODEL), F32),
        scratch_shapes=[
            pltpu.VMEM((HEADS, nb, HEAD_DIM), F32),
            pltpu.VMEM((HEADS, MOBA_BLOCK, HEAD_DIM), F32),
            pltpu.VMEM((2, SEQ, MOBA_BLOCK), F32),
        ],
        compiler_params=pltpu.CompilerParams(
            dimension_semantics=("arbitrary", "arbitrary"), vmem_limit_bytes=VMEM_LIMIT_BYTES),
        name="moba_attn",
    )(q, k, vt, gate, x2d, w_out.astype(BF16), ln_g.reshape(1, D_MODEL), ln_b.reshape(1, D_MODEL))


def _s5_discretize_kernel(ar_ref, ai_ref, ldt_ref, br_ref, bi_ref, abr_ref, abi_ref, bbr_ref, bbi_ref):
    ar, ai = ar_ref[...], ai_ref[...]
    dt = jnp.exp(ldt_ref[...])
    mag = jnp.exp(dt * ar)
    abr = mag * jnp.cos(dt * ai)
    abi = mag * jnp.sin(dt * ai)
    nr, ni = abr - 1.0, abi
    den = ar * ar + ai * ai
    zr = (nr * ar + ni * ai) / den
    zi = (ni * ar - nr * ai) / den
    br, bi = br_ref[...], bi_ref[...]
    abr_ref[...] = abr
    abi_ref[...] = abi
    bbr_ref[...] = zr * br - zi * bi
    bbi_ref[...] = zr * bi + zi * br


def _s5_kernel(x_ref, w_in_ref, bd_ref, lam_ref, cd_ref, d_ref, w_glu_ref, b_glu_ref, w_out_ref,
               lng_ref, lnb_ref, out_ref, hst_ref, xs_ref, hb_ref, y_ref, xt_ref, ot_ref):
    ns = S5_SLAB_STATES
    lane_slabs = D_MODEL // LANES

    @pl.when(pl.program_id(0) == 0)
    def _():
        hst_ref[...] = jnp.zeros_like(hst_ref)

    half_steps = S5_STEPS // 2
    half_rows = half_steps * BATCH
    x_halves, z_halves = [], []
    for th in range(2):
        for b in range(BATCH):
            for ls in range(lane_slabs):
                col = b * D_MODEL + ls * LANES
                xt_ref[ls, pl.ds(th * half_rows + b, half_steps, stride=BATCH), :] = \
                    x_ref[th * half_steps:(th + 1) * half_steps, col:col + LANES]
        xh = jnp.concatenate([xt_ref[ls, th * half_rows:(th + 1) * half_rows, :] for ls in range(lane_slabs)],
                             axis=1)
        x_halves.append(xh)
        z_halves.append(_dot(xh.astype(BF16), w_in_ref[...]))
    z = jnp.concatenate(z_halves, axis=0)
    u = z[:, 0:D_MODEL]
    gate = z[:, D_MODEL:2 * D_MODEL]
    u_bf = u.astype(BF16)

    for s in range(S5_NSLAB):
        xs_ref[...] = _dot(u_bf[:, s * S5_SLAB:(s + 1) * S5_SLAB], bd_ref[s])
        lr = jnp.broadcast_to(lam_ref[s, 0:1, :], (BATCH, ns))
        li = jnp.broadcast_to(lam_ref[s, 1:2, :], (BATCH, ns))

        def step(t, h):
            hr, hi = h
            r = pl.multiple_of(t * BATCH, BATCH)
            xr = xs_ref[pl.ds(r, BATCH), 0:ns]
            xi = xs_ref[pl.ds(r, BATCH), ns:2 * ns]
            nhr = lr * hr - li * hi + xr
            nhi = lr * hi + li * hr + xi
            hb_ref[pl.ds(r, BATCH), 0:ns] = nhr.astype(BF16)
            hb_ref[pl.ds(r, BATCH), ns:2 * ns] = nhi.astype(BF16)
            return nhr, nhi

        hr, hi = lax.fori_loop(0, S5_STEPS, step, (hst_ref[s, 0], hst_ref[s, 1]), unroll=True)
        hst_ref[s, 0] = hr
        hst_ref[s, 1] = hi
        y_ref[:, s * S5_SLAB:(s + 1) * S5_SLAB] = _dot(hb_ref[...], cd_ref[s])

    y = y_ref[...] + d_ref[...] * u
    y = jax.nn.gelu(y)
    y = y * jax.nn.sigmoid(_dot(y.astype(BF16), w_glu_ref[...]) + b_glu_ref[...])
    y = y * jax.nn.silu(gate)
    y_bf = y.astype(BF16)
    for th in range(2):
        hrows = slice(th * half_rows, (th + 1) * half_rows)
        out = _dot(y_bf[hrows], w_out_ref[...])
        res = _residual_layer_norm(x_halves[th], out, lng_ref[...], lnb_ref[...])
        for ls in range(lane_slabs):
            ot_ref[ls, hrows, :] = res[:, ls * LANES:(ls + 1) * LANES]
        for b in range(BATCH):
            for ls in range(lane_slabs):
                col = b * D_MODEL + ls * LANES
                out_ref[th * half_steps:(th + 1) * half_steps, col:col + LANES] = \
                    ot_ref[ls, pl.ds(th * half_rows + b, half_steps, stride=BATCH), :]


def _s5_layer(x_sbd, w_in, a_re, a_im, log_dt, b_re, b_im, c_re, c_im, d_skip, w_glu, b_glu, w_out, ln_g, ln_b):
    gh = S5_GROUPS * S5_GROUP
    rep = lambda a: jnp.repeat(a, S5_GROUP, axis=0)
    small = jax.ShapeDtypeStruct((gh, S5_STATE), F32)
    abr, abi, bbr, bbi = pl.pallas_call(
        _s5_discretize_kernel,
        out_shape=[small] * 4,
        name="s5_discretize",
    )(rep(a_re), rep(a_im), rep(jnp.broadcast_to(log_dt[:, None], (S5_GROUPS, S5_STATE))),
      b_re.transpose(0, 2, 1).reshape(gh, S5_STATE), b_im.transpose(0, 2, 1).reshape(gh, S5_STATE))

    gl = S5_SLAB // S5_GROUP
    eye = jnp.eye(gl, dtype=F32)

    def block_diag_in(bb):
        t = bb.reshape(S5_NSLAB, gl, S5_GROUP, S5_STATE)
        return jnp.einsum('sghp,gk->sghkp', t, eye).reshape(S5_NSLAB, S5_SLAB, gl * S5_STATE)

    def block_diag_out(cc):
        t = cc.reshape(S5_NSLAB, gl, S5_GROUP, S5_STATE)
        return jnp.einsum('sghp,gk->skpgh', t, eye).reshape(S5_NSLAB, gl * S5_STATE, S5_SLAB)

    bd = jnp.concatenate([block_diag_in(bbr), block_diag_in(bbi)], axis=2).astype(BF16)
    cd = jnp.concatenate([block_diag_out(c_re), -block_diag_out(c_im)], axis=1).astype(BF16)
    lam = jnp.stack([abr[::S5_GROUP].reshape(S5_NSLAB, gl * S5_STATE),
                     abi[::S5_GROUP].reshape(S5_NSLAB, gl * S5_STATE)], axis=1)

    rows = S5_STEPS * BATCH
    ns2 = 2 * S5_SLAB_STATES
    row_spec = pl.BlockSpec((S5_STEPS, BATCH * D_MODEL), lambda i: (i, 0))
    return pl.pallas_call(
        _s5_kernel,
        grid=(SEQ // S5_STEPS,),
        in_specs=[
            row_spec,
            _const_spec((D_MODEL, 2 * D_MODEL)),
            _const_spec((S5_NSLAB, S5_SLAB, ns2)),
            _const_spec((S5_NSLAB, 2, S5_SLAB_STATES)),
            _const_spec((S5_NSLAB, ns2, S5_SLAB)),
            _const_spec((1, D_MODEL)),
            _const_spec((D_MODEL, D_MODEL)),
            _const_spec((1, D_MODEL)),
            _const_spec((D_MODEL, D_MODEL)),
            _const_spec((1, D_MODEL)),
            _const_spec((1, D_MODEL)),
        ],
        out_specs=row_spec,
        out_shape=jax.ShapeDtypeStruct((SEQ, BATCH * D_MODEL), F32),
        scratch_shapes=[
            pltpu.VMEM((S5_NSLAB, 2, BATCH, S5_SLAB_STATES), F32),
            pltpu.VMEM((rows, ns2), F32),
            pltpu.VMEM((rows, ns2), BF16),
            pltpu.VMEM((rows, D_MODEL), F32),
            pltpu.VMEM((D_MODEL // LANES, rows, LANES), F32),
            pltpu.VMEM((D_MODEL // LANES, rows, LANES), F32),
        ],
        compiler_params=pltpu.CompilerParams(
            dimension_semantics=("arbitrary",), vmem_limit_bytes=VMEM_LIMIT_BYTES),
        name="s5_layer",
    )(x_sbd, w_in.astype(BF16), bd, lam, cd, d_skip.reshape(1, D_MODEL), w_glu.astype(BF16),
      b_glu.reshape(1, D_MODEL), w_out.astype(BF16), ln_g.reshape(1, D_MODEL), ln_b.reshape(1, D_MODEL))


def kernel(x, hgrn_lower_bounds,
           l0_w_in, l0_norm_g, l0_w_out, l0_ln_g, l0_ln_b,
           l1_w_in, l1_w_out, l1_ln_g, l1_ln_b,
           l2_w_in, l2_a_re, l2_a_im, l2_log_dt, l2_b_re, l2_b_im, l2_c_re, l2_c_im,
           l2_d, l2_w_glu, l2_b_glu, l2_w_out, l2_ln_g, l2_ln_b,
           l3_w_in, l3_norm_g, l3_w_out, l3_ln_g, l3_ln_b):
    h = x.reshape(BATCH * SEQ, D_MODEL)
    h = _hgrn_layer(h, hgrn_lower_bounds, l0_w_in, l0_norm_g, l0_w_out, l0_ln_g, l0_ln_b,
                    layer=0, in_tb=False, out_tb=False)
    h = _moba_layer(h, l1_w_in, l1_w_out, l1_ln_g, l1_ln_b)
    h = _s5_layer(h, l2_w_in, l2_a_re, l2_a_im, l2_log_dt, l2_b_re, l2_b_im,
                  l2_c_re, l2_c_im, l2_d, l2_w_glu, l2_b_glu, l2_w_out, l2_ln_g, l2_ln_b)
    h = _hgrn_layer(h, hgrn_lower_bounds, l3_w_in, l3_norm_g, l3_w_out,
                    l3_ln_g, l3_ln_b, layer=3, in_tb=True, out_tb=False)
    return h.reshape(BATCH, SEQ, D_MODEL)
```

```python
import functools
import math

import numpy as np
import jax
import jax.numpy as jnp
from jax import lax
from jax.experimental import pallas as pl
from jax.experimental.pallas import tpu as pltpu

D_MODEL = 1024
BATCH = 16
SEQ = 2048
DEPTH = 4
HEADS = 8
HEAD_DIM = 128
MOBA_BLOCK = 256
MOBA_TOPK = 3
ROPE_THETA = 10000.0
S5_GROUP = 16
S5_GROUPS = 64
S5_STATE = 64
ALPHA = (2 * DEPTH) ** 0.25
LN_EPS = 1e-5
RMS_EPS = 1e-6
NEG = -1e30

VMEM_LIMIT_BYTES = 56 * 1024 * 1024
LANES = 128

HGRN_TILE = 256
HGRN_CHUNK = 128
PROJ_TILE = 512
S5_STEPS = 32
S5_SLAB = 256
S5_NSLAB = D_MODEL // S5_SLAB
S5_SLAB_STATES = (S5_SLAB // S5_GROUP) * S5_STATE

F32 = jnp.float32
BF16 = jnp.bfloat16


def _dot(a, b):
    return jnp.dot(a, b, preferred_element_type=F32)


def _dot_nt(a, b):
    return lax.dot_general(a, b, (((1,), (1,)), ((), ())), preferred_element_type=F32)


def _split2(a):
    hi = a.astype(BF16)
    lo = (a - hi.astype(F32)).astype(BF16)
    return hi, lo


def _block_diag(a, b):
    zero = jnp.zeros_like(a)
    return jnp.concatenate([jnp.concatenate([a, zero], axis=1), jnp.concatenate([zero, b], axis=1)], axis=0)


def _residual_layer_norm(x, y, g, b):
    r = ALPHA * x + y
    mu = jnp.mean(r, axis=-1, keepdims=True)
    c = r - mu
    var = jnp.mean(c * c, axis=-1, keepdims=True)
    return c * lax.rsqrt(var + LN_EPS) * g + b


def _const_spec(shape):
    nd = len(shape)
    return pl.BlockSpec(shape, lambda *_: (0,) * nd, pipeline_mode=pl.Buffered(1))


def _hgrn_level_table(c):
    t = np.arange(c)[:, None]
    s = np.arange(c)[None, :]
    x = np.bitwise_xor(t, s)
    lv = np.zeros((c, c), np.int32)
    nz = x > 0
    lv[nz] = 2 ** (np.floor(np.log2(x[nz])).astype(np.int32) + 1)
    lv[s >= t] = 0
    return lv


def _segment_mid_rows(bc, seg):
    c, w = bc.shape
    half = seg // 2
    row = lax.broadcasted_iota(jnp.int32, (8, w), 0)
    pieces = []
    for a in range(0, c, 8):
        cand = [jnp.broadcast_to(bc[a + o + half - 1:a + o + half, :], (8, w)) for o in range(0, 8, seg)]
        out = cand[-1]
        for idx in range(len(cand) - 2, -1, -1):
            out = jnp.where(row < (idx + 1) * seg, cand[idx], out)
        pieces.append(out)
    return jnp.concatenate(pieces, axis=0)


def _level_operand(q, k, bc, seg, second):
    c, _ = bc.shape
    half = seg // 2
    if half >= 8:
        pieces = []
        for a in range(0, c, seg):
            m = bc[a + half - 1:a + half, :]
            pieces.append(k[a:a + half] * jnp.exp2(m - bc[a:a + half]))
            pieces.append(q[a + half:a + seg] * jnp.exp2(bc[a + half:a + seg] - m))
        return jnp.concatenate(pieces, axis=0)
    decay = jnp.exp2(-jnp.abs(bc - _segment_mid_rows(bc, seg)))
    return jnp.where(second, q, k) * decay


HGRN_PROJ_PIECE = 256


def _hgrn_kernel(layer, x_ref, xn_ref, lbl_ref, lv_ref, tri_ref, w_in_ref, ng_ref, w_out_ref, lng_ref, lnb_ref,
                 out_ref, st_ref, o_ref, xb_ref, za_ref, zb_ref):
    t = HGRN_TILE

    def proj_pieces(src, slot, z_ref):
        def cast():
            xb_ref[slot] = src[...].astype(BF16)

        def piece(j):
            cols = slice(j * HGRN_PROJ_PIECE, (j + 1) * HGRN_PROJ_PIECE)
            z_ref[:, cols] = _dot(xb_ref[slot], w_in_ref[:, cols])

        return [cast] + [functools.partial(piece, j) for j in range(4 * D_MODEL // HGRN_PROJ_PIECE)]

    @pl.when((pl.program_id(0) == 0) & (pl.program_id(1) == 0))
    def _():
        for job in proj_pieces(x_ref.at[0:t, :], 0, za_ref):
            job()

    @pl.when(pl.program_id(1) == 0)
    def _():
        st_ref[...] = jnp.zeros_like(st_ref)

    lbl = lbl_ref[...]
    e = jnp.exp(lbl - jnp.max(lbl, axis=0, keepdims=True))
    sm = e / jnp.sum(e, axis=0, keepdims=True)
    lb = jnp.zeros((1, D_MODEL), F32)
    for r in range(1, layer + 1):
        lb = lb + sm[r:r + 1, :]

    mix = functools.partial(_hgrn_mixer, lb, lv_ref, tri_ref, ng_ref, w_out_ref, lng_ref, lnb_ref, st_ref)
    out_ref[0:t, :] = mix(za_ref, x_ref.at[0:t, :], o_ref.at[0], proj_pieces(x_ref.at[t:2 * t, :], 0, zb_ref))
    out_ref[t:2 * t, :] = mix(zb_ref, x_ref.at[t:2 * t, :], o_ref.at[1], proj_pieces(xn_ref, 1, za_ref))


def _hgrn_mixer(lb, lv_ref, tri_ref, ng_ref, w_out_ref, lng_ref, lnb_ref, st_ref, z_ref, x_ref, o_ref, side_jobs):
    c = HGRN_CHUNK
    side_jobs = list(side_jobs)

    def run_side_job():
        if side_jobs:
            side_jobs.pop(0)()

    lv2 = lv_ref[...]
    tri2 = tri_ref[...]
    ng = ng_ref[...]
    row_in_chunk = lax.broadcasted_iota(jnp.int32, (c, D_MODEL), 0)

    for ci in range(HGRN_TILE // c):
        rows = slice(ci * c, (ci + 1) * c)
        q = jax.nn.silu(z_ref[rows, 0:D_MODEL])
        f = lb + (1.0 - lb) * jax.nn.sigmoid(z_ref[rows, D_MODEL:2 * D_MODEL])
        g = jnp.log2(f)
        k = 1.0 - f
        v = z_ref[rows, 2 * D_MODEL:3 * D_MODEL]
        gate = jax.nn.silu(z_ref[rows, 3 * D_MODEL:4 * D_MODEL])
        run_side_job()
        bc = _dot(tri2, jnp.concatenate(_split2(g), axis=0))
        run_side_job()

        w_levels = []
        seg = c
        while seg > 2:
            second = (row_in_chunk & (seg - 1)) >= (seg // 2)
            w_levels.append((seg, _level_operand(q, k, bc, seg, second).astype(BF16)))
            seg //= 2
        odd = (row_in_chunk & 1) == 1
        w_levels.append((2, jnp.where(odd, q * f, k).astype(BF16)))

        qd = (q * jnp.exp2(bc)).astype(BF16)
        b_last = bc[c - 1:c, :]
        kd = (k * jnp.exp2(b_last - bc)).astype(BF16)
        d_last = jnp.exp2(b_last)
        qk = q * k
        run_side_job()

        for hp in range(HEADS // 2):
            pair = slice(2 * hp * HEAD_DIM, 2 * (hp + 1) * HEAD_DIM)
            scores2 = jnp.zeros((c, 2 * c), F32)
            for seg, w in w_levels:
                w2 = w[:, pair]
                scores2 = jnp.where(lv2 == seg, _dot_nt(w2, _block_diag(w2[:, :HEAD_DIM], w2[:, HEAD_DIM:])), scores2)
            st2 = st_ref[:, pair]
            vt = [v[:, pair][:, i * HEAD_DIM:(i + 1) * HEAD_DIM].T.astype(BF16) for i in range(2)]
            kd2 = kd[:, pair]
            upd = _dot(jnp.concatenate(vt, axis=1), _block_diag(kd2[:, :HEAD_DIM], kd2[:, HEAD_DIM:]))
            st_ref[:, pair] = st2 * d_last[:, pair] + upd
            for i in range(2):
                sl = slice((2 * hp + i) * HEAD_DIM, (2 * hp + i + 1) * HEAD_DIM)
                lhs = jnp.concatenate([scores2[:, i * c:(i + 1) * c].astype(BF16), qd[:, sl]], axis=1)
                rhs_t = jnp.concatenate([vt[i], st2[:, i * HEAD_DIM:(i + 1) * HEAD_DIM].astype(BF16)], axis=1)
                o = _dot_nt(lhs, rhs_t) + jnp.sum(qk[:, sl], axis=-1, keepdims=True) * v[:, sl]
                o = o * lax.rsqrt(jnp.mean(o * o, axis=-1, keepdims=True) + RMS_EPS) * ng[:, sl]
                o_ref[rows, sl] = (o * gate[:, sl]).astype(BF16)
                run_side_job()

    while side_jobs:
        run_side_job()
    y = _dot(o_ref[...], w_out_ref[...])
    return _residual_layer_norm(x_ref[...], y, lng_ref[...], lnb_ref[...])


def _hgrn_layer(x2d, lb_logits, w_in, norm_g, w_out, ln_g, ln_b, *, layer, in_tb, out_tb):
    nt = SEQ // HGRN_TILE
    ns = nt // 2
    pair = (2 * HGRN_TILE, D_MODEL)
    tile = (HGRN_TILE, D_MODEL)
    tb_spec = pl.BlockSpec(pair, lambda b, c: (c, b))
    bt_spec = pl.BlockSpec(pair, lambda b, c: (b * ns + c, 0))

    def next_tile(b, c):
        return jnp.minimum(2 * (b * ns + c) + 2, BATCH * nt - 1)

    tb_next = pl.BlockSpec(tile, lambda b, c: (next_tile(b, c) % nt, next_tile(b, c) // nt))
    bt_next = pl.BlockSpec(tile, lambda b, c: (next_tile(b, c), 0))
    out_shape = (SEQ, BATCH * D_MODEL) if out_tb else (BATCH * SEQ, D_MODEL)
    lv = _hgrn_level_table(HGRN_CHUNK)
    lv2 = jnp.asarray(np.concatenate([lv, lv], axis=1))
    tri = np.tril(np.ones((HGRN_CHUNK, HGRN_CHUNK), np.float32))
    tri2 = jnp.asarray(np.concatenate([tri, tri], axis=1), dtype=BF16)
    return pl.pallas_call(
        functools.partial(_hgrn_kernel, layer),
        grid=(BATCH, ns),
        in_specs=[
            tb_spec if in_tb else bt_spec,
            tb_next if in_tb else bt_next,
            _const_spec((DEPTH, D_MODEL)),
            _const_spec((HGRN_CHUNK, 2 * HGRN_CHUNK)),
            _const_spec((HGRN_CHUNK, 2 * HGRN_CHUNK)),
            _const_spec((D_MODEL, 4 * D_MODEL)),
            _const_spec((1, D_MODEL)),
            _const_spec((D_MODEL, D_MODEL)),
            _const_spec((1, D_MODEL)),
            _const_spec((1, D_MODEL)),
        ],
        out_specs=tb_spec if out_tb else bt_spec,
        out_shape=jax.ShapeDtypeStruct(out_shape, F32),
        scratch_shapes=[
            pltpu.VMEM((HEAD_DIM, D_MODEL), F32),
            pltpu.VMEM((2, HGRN_TILE, D_MODEL), BF16),
            pltpu.VMEM((2, HGRN_TILE, D_MODEL), BF16),
            pltpu.VMEM((HGRN_TILE, 4 * D_MODEL), F32),
            pltpu.VMEM((HGRN_TILE, 4 * D_MODEL), F32),
        ],
        compiler_params=pltpu.CompilerParams(
            dimension_semantics=("arbitrary", "arbitrary"),
            vmem_limit_bytes=VMEM_LIMIT_BYTES),
        name=f"hgrn2_layer{layer}",
    )(x2d, x2d, lb_logits, lv2, tri2, w_in.astype(BF16), norm_g.reshape(1, D_MODEL), w_out.astype(BF16),
      ln_g.reshape(1, D_MODEL), ln_b.reshape(1, D_MODEL))


def _moba_proj_kernel(x_ref, w_in_ref, cq_ref, sq_ref, ck_ref, sk_ref, q_ref, k_ref, vt_ref, gate_ref):
    z = _dot(x_ref[...].astype(BF16), w_in_ref[...])
    cq, sq, ck, sk = cq_ref[...], sq_ref[...], ck_ref[...], sk_ref[...]
    for h in range(HEADS):
        qh = z[:, h * HEAD_DIM:(h + 1) * HEAD_DIM]
        kh = z[:, D_MODEL + h * HEAD_DIM:D_MODEL + (h + 1) * HEAD_DIM]
        vh = z[:, 2 * D_MODEL + h * HEAD_DIM:2 * D_MODEL + (h + 1) * HEAD_DIM]
        q_ref[0, h] = (qh * cq + pltpu.roll(qh, HEAD_DIM // 2, 1) * sq).astype(BF16)
        k_ref[0, h] = (kh * ck + pltpu.roll(kh, HEAD_DIM // 2, 1) * sk).astype(BF16)
        vt_ref[0, h] = vh.T.astype(BF16)
    gate_ref[...] = jax.nn.silu(z[:, 3 * D_MODEL:4 * D_MODEL]).astype(BF16)


def _moba_attn_kernel(q_ref, k_ref, vt_ref, gate_ref, x_ref, w_out_ref, lng_ref, lnb_ref,
                      out_ref, km_ref, o_ref, s_ref):
    n = pl.program_id(1)
    blk = MOBA_BLOCK
    nb = SEQ // blk

    @pl.when(n == 0)
    def _():
        for h in range(HEADS):
            kf = k_ref[0, h].astype(F32).reshape(nb, blk, HEAD_DIM)
            km_ref[h] = jnp.mean(kf, axis=1)

    key = lax.broadcasted_iota(jnp.int32, (blk, blk), 0)
    qry = lax.broadcasted_iota(jnp.int32, (blk, blk), 1)
    causal = key <= qry

    def attend(n_past):
        blocks = [slice(j * blk, (j + 1) * blk) for j in range(n_past + 1)]

        def selection(h, q):
            if n_past <= MOBA_TOPK:
                return None
            km = km_ref[h]
            km_hi = km.astype(BF16)
            km_lo = (km - km_hi.astype(F32)).astype(BF16)
            gsc = _dot_nt(km_hi, q) + _dot_nt(km_lo, q)
            g = [gsc[j:j + 1, :] for j in range(n_past)]
            sel = []
            for j in range(n_past):
                cnt = jnp.zeros((1, blk), F32)
                for i in range(n_past):
                    if i != j:
                        beats = (g[i] >= g[j]) if i < j else (g[i] > g[j])
                        cnt = cnt + beats.astype(F32)
                sel.append(cnt < MOBA_TOPK)
            return sel

        def score_block(h, q, sel, j, slot):
            s = _dot_nt(k_ref[0, h, blocks[j], :], q)
            if j == n_past:
                s = jnp.where(causal, s, NEG)
            elif sel is not None:
                s = jnp.where(sel[j], s, NEG)
            s_ref[slot, blocks[j], :] = s
            return jnp.max(s, axis=0, keepdims=True)

        def value_block(h, m, j, slot):
            p = jnp.exp2(s_ref[slot, blocks[j], :] - m)
            return _dot(vt_ref[0, h, :, blocks[j]], p.astype(BF16)), jnp.sum(p, axis=0, keepdims=True)

        def stage(h_cur, m_cur, h_next):
            if h_next is not None:
                q = q_ref[0, h_next]
                sel = selection(h_next, q)
            m_next = acc = l = None
            for j in range(n_past + 1):
                if h_next is not None:
                    mj = score_block(h_next, q, sel, j, h_next % 2)
                    m_next = mj if m_next is None else jnp.maximum(m_next, mj)
                if h_cur is not None:
                    aj, lj = value_block(h_cur, m_cur, j, h_cur % 2)
                    acc = aj if acc is None else acc + aj
                    l = lj if l is None else l + lj
            if h_cur is not None:
                o_ref[h_cur] = (acc * (1.0 / l)).T
            return m_next

        m = stage(None, None, 0)
        for h in range(HEADS):
            m = stage(h, m, h + 1 if h + 1 < HEADS else None)

    for n_past in range(nb):
        pl.when(n == n_past)(functools.partial(attend, n_past))

    o = jnp.concatenate([o_ref[h] for h in range(HEADS)], axis=1)
    o = (o * gate_ref[...].astype(F32)).astype(BF16)
    y = _dot(o, w_out_ref[...])
    out_ref[...] = _residual_layer_norm(x_ref[...], y, lng_ref[...], lnb_ref[...])


def _rope_tables():
    pos = np.arange(SEQ, dtype=np.float32)
    inv_freq = (1.0 / (np.float32(ROPE_THETA) ** (np.arange(0, HEAD_DIM, 2, dtype=np.float32) / HEAD_DIM))
                ).astype(np.float32)
    ang = pos[:, None] * inv_freq[None, :]
    cos, sin = np.cos(ang).astype(np.float32), np.sin(ang).astype(np.float32)
    c = np.concatenate([cos, cos], axis=1)
    s = np.concatenate([-sin, sin], axis=1)
    scale = np.float32(HEAD_DIM ** -0.5 * math.log2(math.e))
    return c * scale, s * scale, c, s


def _moba_layer(x2d, w_in, w_out, ln_g, ln_b):
    tiles_per_seq = SEQ // PROJ_TILE
    hshape = (BATCH, HEADS, SEQ, HEAD_DIM)
    hspec = pl.BlockSpec((1, HEADS, PROJ_TILE, HEAD_DIM), lambda i: (i // tiles_per_seq, 0, i % tiles_per_seq, 0))
    tspec = pl.BlockSpec((PROJ_TILE, HEAD_DIM), lambda i: (i % tiles_per_seq, 0))
    nb = SEQ // MOBA_BLOCK
    vt_shape = (BATCH, HEADS, HEAD_DIM, SEQ)
    vt_spec = pl.BlockSpec((1, HEADS, HEAD_DIM, PROJ_TILE),
                           lambda i: (i // tiles_per_seq, 0, 0, i % tiles_per_seq))
    cq, sq, ck, sk = (jnp.asarray(t) for t in _rope_tables())
    q, k, vt, gate = pl.pallas_call(
        _moba_proj_kernel,
        grid=(BATCH * tiles_per_seq,),
        in_specs=[
            pl.BlockSpec((PROJ_TILE, D_MODEL), lambda i: (i, 0)),
            _const_spec((D_MODEL, 4 * D_MODEL)),
            tspec, tspec, tspec, tspec,
        ],
        out_specs=[hspec, hspec, vt_spec, pl.BlockSpec((PROJ_TILE, D_MODEL), lambda i: (i, 0))],
        out_shape=[jax.ShapeDtypeStruct(hshape, BF16)] * 2 + [jax.ShapeDtypeStruct(vt_shape, BF16),
                                                              jax.ShapeDtypeStruct((BATCH * SEQ, D_MODEL), BF16)],
        compiler_params=pltpu.CompilerParams(
            dimension_semantics=("arbitrary",), vmem_limit_bytes=VMEM_LIMIT_BYTES),
        name="moba_proj",
    )(x2d, w_in.astype(BF16), cq, sq, ck, sk)

    k_spec = pl.BlockSpec((1, HEADS, SEQ, HEAD_DIM), lambda b, n: (b, 0, 0, 0))
    vt_in_spec = pl.BlockSpec((1, HEADS, HEAD_DIM, SEQ), lambda b, n: (b, 0, 0, 0))
    row_spec = pl.BlockSpec((MOBA_BLOCK, D_MODEL), lambda b, n: (b * nb + n, 0))
    return pl.pallas_call(
        _moba_attn_kernel,
        grid=(BATCH, nb),
        in_specs=[
            pl.BlockSpec((1, HEADS, MOBA_BLOCK, HEAD_DIM), lambda b, n: (b, 0, n, 0)),
            k_spec, vt_in_spec, row_spec, row_spec,
            _const_spec((D_MODEL, D_MODEL)),
            _const_spec((1, D_MODEL)),
            _const_spec((1, D_MODEL)),
        ],
        out_specs=pl.BlockSpec((MOBA_BLOCK, D_MODEL), lambda b, n: (n, b)),
        out_shape=jax.ShapeDtypeStruct((SEQ, BATCH * D_MODEL), F32),
        scratch_shapes=[
            pltpu.VMEM((HEADS, nb, HEAD_DIM), F32),
            pltpu.VMEM((HEADS, MOBA_BLOCK, HEAD_DIM), F32),
            pltpu.VMEM((2, SEQ, MOBA_BLOCK), F32),
        ],
        compiler_params=pltpu.CompilerParams(
            dimension_semantics=("arbitrary", "arbitrary"), vmem_limit_bytes=VMEM_LIMIT_BYTES),
        name="moba_attn",
    )(q, k, vt, gate, x2d, w_out.astype(BF16), ln_g.reshape(1, D_MODEL), ln_b.reshape(1, D_MODEL))


def _s5_discretize_kernel(ar_ref, ai_ref, ldt_ref, br_ref, bi_ref, abr_ref, abi_ref, bbr_ref, bbi_ref):
    ar, ai = ar_ref[...], ai_ref[...]
    dt = jnp.exp(ldt_ref[...])
    mag = jnp.exp(dt * ar)
    abr = mag * jnp.cos(dt * ai)
    abi = mag * jnp.sin(dt * ai)
    nr, ni = abr - 1.0, abi
    den = ar * ar + ai * ai
    zr = (nr * ar + ni * ai) / den
    zi = (ni * ar - nr * ai) / den
    br, bi = br_ref[...], bi_ref[...]
    abr_ref[...] = abr
    abi_ref[...] = abi
    bbr_ref[...] = zr * br - zi * bi
    bbi_ref[...] = zr * bi + zi * br


def _s5_kernel(x_ref, w_in_ref, bd_ref, lam_ref, cd_ref, d_ref, w_glu_ref, b_glu_ref, w_out_ref,
               lng_ref, lnb_ref, out_ref, hst_ref, xs_ref, hb_ref, y_ref, xt_ref, ot_ref):
    ns = S5_SLAB_STATES
    lane_slabs = D_MODEL // LANES

    @pl.when(pl.program_id(0) == 0)
    def _():
        hst_ref[...] = jnp.zeros_like(hst_ref)

    half_steps = S5_STEPS // 2
    half_rows = half_steps * BATCH
    x_halves, z_halves = [], []
    for th in range(2):
        for b in range(BATCH):
            for ls in range(lane_slabs):
                col = b * D_MODEL + ls * LANES
                xt_ref[ls, pl.ds(th * half_rows + b, half_steps, stride=BATCH), :] = \
                    x_ref[th * half_steps:(th + 1) * half_steps, col:col + LANES]
        xh = jnp.concatenate([xt_ref[ls, th * half_rows:(th + 1) * half_rows, :] for ls in range(lane_slabs)],
                             axis=1)
        x_halves.append(xh)
        z_halves.append(_dot(xh.astype(BF16), w_in_ref[...]))
    z = jnp.concatenate(z_halves, axis=0)
    u = z[:, 0:D_MODEL]
    gate = z[:, D_MODEL:2 * D_MODEL]
    u_bf = u.astype(BF16)

    for s in range(S5_NSLAB):
        xs_ref[...] = _dot(u_bf[:, s * S5_SLAB:(s + 1) * S5_SLAB], bd_ref[s])
        lr = jnp.broadcast_to(lam_ref[s, 0:1, :], (BATCH, ns))
        li = jnp.broadcast_to(lam_ref[s, 1:2, :], (BATCH, ns))

        def step(t, h):
            hr, hi = h
            r = pl.multiple_of(t * BATCH, BATCH)
            xr = xs_ref[pl.ds(r, BATCH), 0:ns]
            xi = xs_ref[pl.ds(r, BATCH), ns:2 * ns]
            nhr = lr * hr - li * hi + xr
            nhi = lr * hi + li * hr + xi
            hb_ref[pl.ds(r, BATCH), 0:ns] = nhr.astype(BF16)
            hb_ref[pl.ds(r, BATCH), ns:2 * ns] = nhi.astype(BF16)
            return nhr, nhi

        hr, hi = lax.fori_loop(0, S5_STEPS, step, (hst_ref[s, 0], hst_ref[s, 1]), unroll=True)
        hst_ref[s, 0] = hr
        hst_ref[s, 1] = hi
        y_ref[:, s * S5_SLAB:(s + 1) * S5_SLAB] = _dot(hb_ref[...], cd_ref[s])

    y = y_ref[...] + d_ref[...] * u
    y = jax.nn.gelu(y)
    y = y * jax.nn.sigmoid(_dot(y.astype(BF16), w_glu_ref[...]) + b_glu_ref[...])
    y = y * jax.nn.silu(gate)
    y_bf = y.astype(BF16)
    for th in range(2):
        hrows = slice(th * half_rows, (th + 1) * half_rows)
        out = _dot(y_bf[hrows], w_out_ref[...])
        res = _residual_layer_norm(x_halves[th], out, lng_ref[...], lnb_ref[...])
        for ls in range(lane_slabs):
            ot_ref[ls, hrows, :] = res[:, ls * LANES:(ls + 1) * LANES]
        for b in range(BATCH):
            for ls in range(lane_slabs):
                col = b * D_MODEL + ls * LANES
                out_ref[th * half_steps:(th + 1) * half_steps, col:col + LANES] = \
                    ot_ref[ls, pl.ds(th * half_rows + b, half_steps, stride=BATCH), :]


def _s5_layer(x_sbd, w_in, a_re, a_im, log_dt, b_re, b_im, c_re, c_im, d_skip, w_glu, b_glu, w_out, ln_g, ln_b):
    gh = S5_GROUPS * S5_GROUP
    rep = lambda a: jnp.repeat(a, S5_GROUP, axis=0)
    small = jax.ShapeDtypeStruct((gh, S5_STATE), F32)
    abr, abi, bbr, bbi = pl.pallas_call(
        _s5_discretize_kernel,
        out_shape=[small] * 4,
        name="s5_discretize",
    )(rep(a_re), rep(a_im), rep(jnp.broadcast_to(log_dt[:, None], (S5_GROUPS, S5_STATE))),
      b_re.transpose(0, 2, 1).reshape(gh, S5_STATE), b_im.transpose(0, 2, 1).reshape(gh, S5_STATE))

    gl = S5_SLAB // S5_GROUP
    eye = jnp.eye(gl, dtype=F32)

    def block_diag_in(bb):
        t = bb.reshape(S5_NSLAB, gl, S5_GROUP, S5_STATE)
        return jnp.einsum('sghp,gk->sghkp', t, eye).reshape(S5_NSLAB, S5_SLAB, gl * S5_STATE)

    def block_diag_out(cc):
        t = cc.reshape(S5_NSLAB, gl, S5_GROUP, S5_STATE)
        return jnp.einsum('sghp,gk->skpgh', t, eye).reshape(S5_NSLAB, gl * S5_STATE, S5_SLAB)

    bd = jnp.concatenate([block_diag_in(bbr), block_diag_in(bbi)], axis=2).astype(BF16)
    cd = jnp.concatenate([block_diag_out(c_re), -block_diag_out(c_im)], axis=1).astype(BF16)
    lam = jnp.stack([abr[::S5_GROUP].reshape(S5_NSLAB, gl * S5_STATE),
                     abi[::S5_GROUP].reshape(S5_NSLAB, gl * S5_STATE)], axis=1)

    rows = S5_STEPS * BATCH
    ns2 = 2 * S5_SLAB_STATES
    row_spec = pl.BlockSpec((S5_STEPS, BATCH * D_MODEL), lambda i: (i, 0))
    return pl.pallas_call(
        _s5_kernel,
        grid=(SEQ // S5_STEPS,),
        in_specs=[
            row_spec,
            _const_spec((D_MODEL, 2 * D_MODEL)),
            _const_spec((S5_NSLAB, S5_SLAB, ns2)),
            _const_spec((S5_NSLAB, 2, S5_SLAB_STATES)),
            _const_spec((S5_NSLAB, ns2, S5_SLAB)),
            _const_spec((1, D_MODEL)),
            _const_spec((D_MODEL, D_MODEL)),
            _const_spec((1, D_MODEL)),
            _const_spec((D_MODEL, D_MODEL)),
            _const_spec((1, D_MODEL)),
            _const_spec((1, D_MODEL)),
        ],
        out_specs=row_spec,
        out_shape=jax.ShapeDtypeStruct((SEQ, BATCH * D_MODEL), F32),
        scratch_shapes=[
            pltpu.VMEM((S5_NSLAB, 2, BATCH, S5_SLAB_STATES), F32),
            pltpu.VMEM((rows, ns2), F32),
            pltpu.VMEM((rows, ns2), BF16),
            pltpu.VMEM((rows, D_MODEL), F32),
            pltpu.VMEM((D_MODEL // LANES, rows, LANES), F32),
            pltpu.VMEM((D_MODEL // LANES, rows, LANES), F32),
        ],
        compiler_params=pltpu.CompilerParams(
            dimension_semantics=("arbitrary",), vmem_limit_bytes=VMEM_LIMIT_BYTES),
        name="s5_layer",
    )(x_sbd, w_in.astype(BF16), bd, lam, cd, d_skip.reshape(1, D_MODEL), w_glu.astype(BF16),
      b_glu.reshape(1, D_MODEL), w_out.astype(BF16), ln_g.reshape(1, D_MODEL), ln_b.reshape(1, D_MODEL))


def kernel(x, hgrn_lower_bounds,
           l0_w_in, l0_norm_g, l0_w_out, l0_ln_g, l0_ln_b,
           l1_w_in, l1_w_out, l1_ln_g, l1_ln_b,
           l2_w_in, l2_a_re, l2_a_im, l2_log_dt, l2_b_re, l2_b_im, l2_c_re, l2_c_im,
           l2_d, l2_w_glu, l2_b_glu, l2_w_out, l2_ln_g, l2_ln_b,
           l3_w_in, l3_norm_g, l3_w_out, l3_ln_g, l3_ln_b):
    h = x.reshape(BATCH * SEQ, D_MODEL)
    h = _hgrn_layer(h, hgrn_lower_bounds, l0_w_in, l0_norm_g, l0_w_out, l0_ln_g, l0_ln_b,
                    layer=0, in_tb=False, out_tb=False)
    h = _moba_layer(h, l1_w_in, l1_w_out, l1_ln_g, l1_ln_b)
    h = _s5_layer(h, l2_w_in, l2_a_re, l2_a_im, l2_log_dt, l2_b_re, l2_b_im,
                  l2_c_re, l2_c_im, l2_d, l2_w_glu, l2_b_glu, l2_w_out, l2_ln_g, l2_ln_b)
    h = _hgrn_layer(h, hgrn_lower_bounds, l3_w_in, l3_norm_g, l3_w_out,
                    l3_ln_g, l3_ln_b, layer=3, in_tb=True, out_tb=False)
    return h.reshape(BATCH, SEQ, D_MODEL)
```

```python
import functools
import math

import numpy as np
import jax
import jax.numpy as jnp
from jax import lax
from jax.experimental import pallas as pl
from jax.experimental.pallas import tpu as pltpu

D_MODEL = 1024
BATCH = 16
SEQ = 2048
DEPTH = 4
HEADS = 8
HEAD_DIM = 128
MOBA_BLOCK = 256
MOBA_TOPK = 3
ROPE_THETA = 10000.0
S5_GROUP = 16
S5_GROUPS = 64
S5_STATE = 64
ALPHA = (2 * DEPTH) ** 0.25
LN_EPS = 1e-5
RMS_EPS = 1e-6
NEG = -1e30

VMEM_LIMIT_BYTES = 56 * 1024 * 1024
LANES = 128

HGRN_TILE = 256
HGRN_CHUNK = 128
PROJ_TILE = 512
S5_STEPS = 32
S5_SLAB = 256
S5_NSLAB = D_MODEL // S5_SLAB
S5_SLAB_STATES = (S5_SLAB // S5_GROUP) * S5_STATE

F32 = jnp.float32
BF16 = jnp.bfloat16


def _dot(a, b):
    return jnp.dot(a, b, preferred_element_type=F32)


def _dot_nt(a, b):
    return lax.dot_general(a, b, (((1,), (1,)), ((), ())), preferred_element_type=F32)


def _split2(a):
    hi = a.astype(BF16)
    lo = (a - hi.astype(F32)).astype(BF16)
    return hi, lo


def _block_diag(a, b):
    zero = jnp.zeros_like(a)
    return jnp.concatenate([jnp.concatenate([a, zero], axis=1), jnp.concatenate([zero, b], axis=1)], axis=0)


def _residual_layer_norm(x, y, g, b):
    r = ALPHA * x + y
    mu = jnp.mean(r, axis=-1, keepdims=True)
    c = r - mu
    var = jnp.mean(c * c, axis=-1, keepdims=True)
    return c * lax.rsqrt(var + LN_EPS) * g + b


def _const_spec(shape):
    nd = len(shape)
    return pl.BlockSpec(shape, lambda *_: (0,) * nd, pipeline_mode=pl.Buffered(1))


def _hgrn_level_table(c):
    t = np.arange(c)[:, None]
    s = np.arange(c)[None, :]
    x = np.bitwise_xor(t, s)
    lv = np.zeros((c, c), np.int32)
    nz = x > 0
    lv[nz] = 2 ** (np.floor(np.log2(x[nz])).astype(np.int32) + 1)
    lv[s >= t] = 0
    return lv


def _segment_mid_rows(bc, seg):
    c, w = bc.shape
    half = seg // 2
    row = lax.broadcasted_iota(jnp.int32, (8, w), 0)
    pieces = []
    for a in range(0, c, 8):
        cand = [jnp.broadcast_to(bc[a + o + half - 1:a + o + half, :], (8, w)) for o in range(0, 8, seg)]
        out = cand[-1]
        for idx in range(len(cand) - 2, -1, -1):
            out = jnp.where(row < (idx + 1) * seg, cand[idx], out)
        pieces.append(out)
    return jnp.concatenate(pieces, axis=0)


def _level_operand(q, k, bc, seg, second):
    c, _ = bc.shape
    half = seg // 2
    if half >= 8:
        pieces = []
        for a in range(0, c, seg):
            m = bc[a + half - 1:a + half, :]
            pieces.append(k[a:a + half] * jnp.exp2(m - bc[a:a + half]))
            pieces.append(q[a + half:a + seg] * jnp.exp2(bc[a + half:a + seg] - m))
        return jnp.concatenate(pieces, axis=0)
    decay = jnp.exp2(-jnp.abs(bc - _segment_mid_rows(bc, seg)))
    return jnp.where(second, q, k) * decay


HGRN_PROJ_PIECE = 256


def _hgrn_kernel(layer, x_ref, xn_ref, lbl_ref, lv_ref, tri_ref, w_in_ref, ng_ref, w_out_ref, lng_ref, lnb_ref,
                 out_ref, st_ref, o_ref, xb_ref, za_ref, zb_ref):
    t = HGRN_TILE

    def proj_pieces(src, slot, z_ref):
        def cast():
            xb_ref[slot] = src[...].astype(BF16)

        def piece(j):
            cols = slice(j * HGRN_PROJ_PIECE, (j + 1) * HGRN_PROJ_PIECE)
            z_ref[:, cols] = _dot(xb_ref[slot], w_in_ref[:, cols])

        return [cast] + [functools.partial(piece, j) for j in range(4 * D_MODEL // HGRN_PROJ_PIECE)]

    @pl.when((pl.program_id(0) == 0) & (pl.program_id(1) == 0))
    def _():
        for job in proj_pieces(x_ref.at[0:t, :], 0, za_ref):
            job()

    @pl.when(pl.program_id(1) == 0)
    def _():
        st_ref[...] = jnp.zeros_like(st_ref)

    lbl = lbl_ref[...]
    e = jnp.exp(lbl - jnp.max(lbl, axis=0, keepdims=True))
    sm = e / jnp.sum(e, axis=0, keepdims=True)
    lb = jnp.zeros((1, D_MODEL), F32)
    for r in range(1, layer + 1):
        lb = lb + sm[r:r + 1, :]

    mix = functools.partial(_hgrn_mixer, lb, lv_ref, tri_ref, ng_ref, w_out_ref, lng_ref, lnb_ref, st_ref)
    out_ref[0:t, :] = mix(za_ref, x_ref.at[0:t, :], o_ref.at[0], proj_pieces(x_ref.at[t:2 * t, :], 0, zb_ref))
    out_ref[t:2 * t, :] = mix(zb_ref, x_ref.at[t:2 * t, :], o_ref.at[1], proj_pieces(xn_ref, 1, za_ref))


def _hgrn_mixer(lb, lv_ref, tri_ref, ng_ref, w_out_ref, lng_ref, lnb_ref, st_ref, z_ref, x_ref, o_ref, side_jobs):
    c = HGRN_CHUNK
    side_jobs = list(side_jobs)

    def run_side_job():
        if side_jobs:
            side_jobs.pop(0)()

    lv2 = lv_ref[...]
    tri2 = tri_ref[...]
    ng = ng_ref[...]
    row_in_chunk = lax.broadcasted_iota(jnp.int32, (c, D_MODEL), 0)

    for ci in range(HGRN_TILE // c):
        rows = slice(ci * c, (ci + 1) * c)
        q = jax.nn.silu(z_ref[rows, 0:D_MODEL])
        f = lb + (1.0 - lb) * jax.nn.sigmoid(z_ref[rows, D_MODEL:2 * D_MODEL])
        g = jnp.log2(f)
        k = 1.0 - f
        v = z_ref[rows, 2 * D_MODEL:3 * D_MODEL]
        gate = jax.nn.silu(z_ref[rows, 3 * D_MODEL:4 * D_MODEL])
        run_side_job()
        bc = _dot(tri2, jnp.concatenate(_split2(g), axis=0))
        run_side_job()

        w_levels = []
        seg = c
        while seg > 2:
            second = (row_in_chunk & (seg - 1)) >= (seg // 2)
            w_levels.append((seg, _level_operand(q, k, bc, seg, second).astype(BF16)))
            seg //= 2
        odd = (row_in_chunk & 1) == 1
        w_levels.append((2, jnp.where(odd, q * f, k).astype(BF16)))

        qd = (q * jnp.exp2(bc)).astype(BF16)
        b_last = bc[c - 1:c, :]
        kd = (k * jnp.exp2(b_last - bc)).astype(BF16)
        d_last = jnp.exp2(b_last)
        qk = q * k
        run_side_job()

        for hp in range(HEADS // 2):
            pair = slice(2 * hp * HEAD_DIM, 2 * (hp + 1) * HEAD_DIM)
            scores2 = jnp.zeros((c, 2 * c), F32)
            for seg, w in w_levels:
                w2 = w[:, pair]
                scores2 = jnp.where(lv2 == seg, _dot_nt(w2, _block_diag(w2[:, :HEAD_DIM], w2[:, HEAD_DIM:])), scores2)
            st2 = st_ref[:, pair]
            vt = [v[:, pair][:, i * HEAD_DIM:(i + 1) * HEAD_DIM].T.astype(BF16) for i in range(2)]
            kd2 = kd[:, pair]
            upd = _dot(jnp.concatenate(vt, axis=1), _block_diag(kd2[:, :HEAD_DIM], kd2[:, HEAD_DIM:]))
            st_ref[:, pair] = st2 * d_last[:, pair] + upd
            for i in range(2):
                sl = slice((2 * hp + i) * HEAD_DIM, (2 * hp + i + 1) * HEAD_DIM)
                lhs = jnp.concatenate([scores2[:, i * c:(i + 1) * c].astype(BF16), qd[:, sl]], axis=1)
                rhs_t = jnp.concatenate([vt[i], st2[:, i * HEAD_DIM:(i + 1) * HEAD_DIM].astype(BF16)], axis=1)
                o = _dot_nt(lhs, rhs_t) + jnp.sum(qk[:, sl], axis=-1, keepdims=True) * v[:, sl]
                o = o * lax.rsqrt(jnp.mean(o * o, axis=-1, keepdims=True) + RMS_EPS) * ng[:, sl]
                o_ref[rows, sl] = (o * gate[:, sl]).astype(BF16)
                run_side_job()

    while side_jobs:
        run_side_job()
    y = _dot(o_ref[...], w_out_ref[...])
    return _residual_layer_norm(x_ref[...], y, lng_ref[...], lnb_ref[...])


def _hgrn_layer(x2d, lb_logits, w_in, norm_g, w_out, ln_g, ln_b, *, layer, in_tb):
    nt = SEQ // HGRN_TILE
    ns = nt // 2
    pair = (2 * HGRN_TILE, D_MODEL)
    tile = (HGRN_TILE, D_MODEL)
    tb_spec = pl.BlockSpec(pair, lambda b, c: (c, b))
    bt_spec = pl.BlockSpec(pair, lambda b, c: (b * ns + c, 0))

    def next_tile(b, c):
        return jnp.minimum(2 * (b * ns + c) + 2, BATCH * nt - 1)

    tb_next = pl.BlockSpec(tile, lambda b, c: (next_tile(b, c) % nt, next_tile(b, c) // nt))
    bt_next = pl.BlockSpec(tile, lambda b, c: (next_tile(b, c), 0))
    lv = _hgrn_level_table(HGRN_CHUNK)
    lv2 = jnp.asarray(np.concatenate([lv, lv], axis=1))
    tri = np.tril(np.ones((HGRN_CHUNK, HGRN_CHUNK), np.float32))
    tri2 = jnp.asarray(np.concatenate([tri, tri], axis=1), dtype=BF16)
    return pl.pallas_call(
        functools.partial(_hgrn_kernel, layer),
        grid=(BATCH, ns),
        in_specs=[
            tb_spec if in_tb else bt_spec,
            tb_next if in_tb else bt_next,
            _const_spec((DEPTH, D_MODEL)),
            _const_spec((HGRN_CHUNK, 2 * HGRN_CHUNK)),
            _const_spec((HGRN_CHUNK, 2 * HGRN_CHUNK)),
            _const_spec((D_MODEL, 4 * D_MODEL)),
            _const_spec((1, D_MODEL)),
            _const_spec((D_MODEL, D_MODEL)),
            _const_spec((1, D_MODEL)),
            _const_spec((1, D_MODEL)),
        ],
        out_specs=bt_spec,
        out_shape=jax.ShapeDtypeStruct((BATCH * SEQ, D_MODEL), F32),
        scratch_shapes=[
            pltpu.VMEM((HEAD_DIM, D_MODEL), F32),
            pltpu.VMEM((2, HGRN_TILE, D_MODEL), BF16),
            pltpu.VMEM((2, HGRN_TILE, D_MODEL), BF16),
            pltpu.VMEM((HGRN_TILE, 4 * D_MODEL), F32),
            pltpu.VMEM((HGRN_TILE, 4 * D_MODEL), F32),
        ],
        compiler_params=pltpu.CompilerParams(
            dimension_semantics=("arbitrary", "arbitrary"),
            vmem_limit_bytes=VMEM_LIMIT_BYTES),
        name=f"hgrn2_layer{layer}",
    )(x2d, x2d, lb_logits, lv2, tri2, w_in.astype(BF16), norm_g.reshape(1, D_MODEL), w_out.astype(BF16),
      ln_g.reshape(1, D_MODEL), ln_b.reshape(1, D_MODEL))


def _moba_proj_kernel(x_ref, w_in_ref, cq_ref, sq_ref, ck_ref, sk_ref, q_ref, k_ref, vt_ref, gate_ref):
    z = _dot(x_ref[...].astype(BF16), w_in_ref[...])
    cq, sq, ck, sk = cq_ref[...], sq_ref[...], ck_ref[...], sk_ref[...]
    for h in range(HEADS):
        qh = z[:, h * HEAD_DIM:(h + 1) * HEAD_DIM]
        kh = z[:, D_MODEL + h * HEAD_DIM:D_MODEL + (h + 1) * HEAD_DIM]
        vh = z[:, 2 * D_MODEL + h * HEAD_DIM:2 * D_MODEL + (h + 1) * HEAD_DIM]
        q_ref[0, h] = (qh * cq + pltpu.roll(qh, HEAD_DIM // 2, 1) * sq).astype(BF16)
        k_ref[0, h] = (kh * ck + pltpu.roll(kh, HEAD_DIM // 2, 1) * sk).astype(BF16)
        vt_ref[0, h] = vh.T.astype(BF16)
    gate_ref[...] = jax.nn.silu(z[:, 3 * D_MODEL:4 * D_MODEL]).astype(BF16)


def _moba_attn_kernel(q_ref, k_ref, vt_ref, gate_ref, x_ref, w_out_ref, lng_ref, lnb_ref,
                      out_ref, km_ref, o_ref, s_ref):
    n = pl.program_id(1)
    blk = MOBA_BLOCK
    nb = SEQ // blk

    @pl.when(n == 0)
    def _():
        for h in range(HEADS):
            kf = k_ref[0, h].astype(F32).reshape(nb, blk, HEAD_DIM)
            km_ref[h] = jnp.mean(kf, axis=1)

    key = lax.broadcasted_iota(jnp.int32, (blk, blk), 0)
    qry = lax.broadcasted_iota(jnp.int32, (blk, blk), 1)
    causal = key <= qry

    def attend(n_past):
        blocks = [slice(j * blk, (j + 1) * blk) for j in range(n_past + 1)]

        def selection(h, q):
            if n_past <= MOBA_TOPK:
                return None
            km = km_ref[h]
            km_hi = km.astype(BF16)
            km_lo = (km - km_hi.astype(F32)).astype(BF16)
            gsc = _dot_nt(km_hi, q) + _dot_nt(km_lo, q)
            g = [gsc[j:j + 1, :] for j in range(n_past)]
            sel = []
            for j in range(n_past):
                cnt = jnp.zeros((1, blk), F32)
                for i in range(n_past):
                    if i != j:
                        beats = (g[i] >= g[j]) if i < j else (g[i] > g[j])
                        cnt = cnt + beats.astype(F32)
                sel.append(cnt < MOBA_TOPK)
            return sel

        def score_block(h, q, sel, j, slot):
            s = _dot_nt(k_ref[0, h, blocks[j], :], q)
            if j == n_past:
                s = jnp.where(causal, s, NEG)
            elif sel is not None:
                s = jnp.where(sel[j], s, NEG)
            s_ref[slot, blocks[j], :] = s
            return jnp.max(s, axis=0, keepdims=True)

        def value_block(h, m, j, slot):
            p = jnp.exp2(s_ref[slot, blocks[j], :] - m)
            return _dot(vt_ref[0, h, :, blocks[j]], p.astype(BF16)), jnp.sum(p, axis=0, keepdims=True)

        def stage(h_cur, m_cur, h_next):
            if h_next is not None:
                q = q_ref[0, h_next]
                sel = selection(h_next, q)
            m_next = acc = l = None
            for j in range(n_past + 1):
                if h_next is not None:
                    mj = score_block(h_next, q, sel, j, h_next % 2)
                    m_next = mj if m_next is None else jnp.maximum(m_next, mj)
                if h_cur is not None:
                    aj, lj = value_block(h_cur, m_cur, j, h_cur % 2)
                    acc = aj if acc is None else acc + aj
                    l = lj if l is None else l + lj
            if h_cur is not None:
                o_ref[h_cur] = (acc * (1.0 / l)).T
            return m_next

        m = stage(None, None, 0)
        for h in range(HEADS):
            m = stage(h, m, h + 1 if h + 1 < HEADS else None)

    for n_past in range(nb):
        pl.when(n == n_past)(functools.partial(attend, n_past))

    o = jnp.concatenate([o_ref[h] for h in range(HEADS)], axis=1)
    o = (o * gate_ref[...].astype(F32)).astype(BF16)
    y = _dot(o, w_out_ref[...])
    out_ref[...] = _residual_layer_norm(x_ref[...], y, lng_ref[...], lnb_ref[...])


def _rope_tables():
    pos = np.arange(SEQ, dtype=np.float32)
    inv_freq = (1.0 / (np.float32(ROPE_THETA) ** (np.arange(0, HEAD_DIM, 2, dtype=np.float32) / HEAD_DIM))
                ).astype(np.float32)
    ang = pos[:, None] * inv_freq[None, :]
    cos, sin = np.cos(ang).astype(np.float32), np.sin(ang).astype(np.float32)
    c = np.concatenate([cos, cos], axis=1)
    s = np.concatenate([-sin, sin], axis=1)
    scale = np.float32(HEAD_DIM ** -0.5 * math.log2(math.e))
    return c * scale, s * scale, c, s


def _moba_layer(x2d, w_in, w_out, ln_g, ln_b):
    tiles_per_seq = SEQ // PROJ_TILE
    hshape = (BATCH, HEADS, SEQ, HEAD_DIM)
    hspec = pl.BlockSpec((1, HEADS, PROJ_TILE, HEAD_DIM), lambda i: (i // tiles_per_seq, 0, i % tiles_per_seq, 0))
    tspec = pl.BlockSpec((PROJ_TILE, HEAD_DIM), lambda i: (i % tiles_per_seq, 0))
    nb = SEQ // MOBA_BLOCK
    vt_shape = (BATCH, HEADS, HEAD_DIM, SEQ)
    vt_spec = pl.BlockSpec((1, HEADS, HEAD_DIM, PROJ_TILE),
                           lambda i: (i // tiles_per_seq, 0, 0, i % tiles_per_seq))
    cq, sq, ck, sk = (jnp.asarray(t) for t in _rope_tables())
    q, k, vt, gate = pl.pallas_call(
        _moba_proj_kernel,
        grid=(BATCH * tiles_per_seq,),
        in_specs=[
            pl.BlockSpec((PROJ_TILE, D_MODEL), lambda i: (i, 0)),
            _const_spec((D_MODEL, 4 * D_MODEL)),
            tspec, tspec, tspec, tspec,
        ],
        out_specs=[hspec, hspec, vt_spec, pl.BlockSpec((PROJ_TILE, D_MODEL), lambda i: (i, 0))],
        out_shape=[jax.ShapeDtypeStruct(hshape, BF16)] * 2 + [jax.ShapeDtypeStruct(vt_shape, BF16),
                                                              jax.ShapeDtypeStruct((BATCH * SEQ, D_MODEL), BF16)],
        compiler_params=pltpu.CompilerParams(
            dimension_semantics=("arbitrary",), vmem_limit_bytes=VMEM_LIMIT_BYTES),
        name="moba_proj",
    )(x2d, w_in.astype(BF16), cq, sq, ck, sk)

    k_spec = pl.BlockSpec((1, HEADS, SEQ, HEAD_DIM), lambda b, n: (b, 0, 0, 0))
    vt_in_spec = pl.BlockSpec((1, HEADS, HEAD_DIM, SEQ), lambda b, n: (b, 0, 0, 0))
    row_spec = pl.BlockSpec((MOBA_BLOCK, D_MODEL), lambda b, n: (b * nb + n, 0))
    return pl.pallas_call(
        _moba_attn_kernel,
        grid=(BATCH, nb),
        in_specs=[
            pl.BlockSpec((1, HEADS, MOBA_BLOCK, HEAD_DIM), lambda b, n: (b, 0, n, 0)),
            k_spec, vt_in_spec, row_spec, row_spec,
            _const_spec((D_MODEL, D_MODEL)),
            _const_spec((1, D_MODEL)),
            _const_spec((1, D_MODEL)),
        ],
        out_specs=pl.BlockSpec((MOBA_BLOCK, D_MODEL), lambda b, n: (n, b)),
        out_shape=jax.ShapeDtypeStruct((SEQ, BATCH * D_MODEL), F32),
        scratch_shapes=[
            pltpu.VMEM((HEADS, nb, HEAD_DIM), F32),
            pltpu.VMEM((HEADS, MOBA_BLOCK, HEAD_DIM), F32),
            pltpu.VMEM((2, SEQ, MOBA_BLOCK), F32),
        ],
        compiler_params=pltpu.CompilerParams(
            dimension_semantics=("arbitrary", "arbitrary"), vmem_limit_bytes=VMEM_LIMIT_BYTES),
        name="moba_attn",
    )(q, k, vt, gate, x2d, w_out.astype(BF16), ln_g.reshape(1, D_MODEL), ln_b.reshape(1, D_MODEL))


def _s5_discretize_kernel(ar_ref, ai_ref, ldt_ref, br_ref, bi_ref, abr_ref, abi_ref, bbr_ref, bbi_ref):
    ar, ai = ar_ref[...], ai_ref[...]
    dt = jnp.exp(ldt_ref[...])
    mag = jnp.exp(dt * ar)
    abr = mag * jnp.cos(dt * ai)
    abi = mag * jnp.sin(dt * ai)
    nr, ni = abr - 1.0, abi
    den = ar * ar + ai * ai
    zr = (nr * ar + ni * ai) / den
    zi = (ni * ar - nr * ai) / den
    br, bi = br_ref[...], bi_ref[...]
    abr_ref[...] = abr
    abi_ref[...] = abi
    bbr_ref[...] = zr * br - zi * bi
    bbi_ref[...] = zr * bi + zi * br


def _s5_kernel(x_ref, w_in_ref, bd_ref, lam_ref, cd_ref, d_ref, w_glu_ref, b_glu_ref, w_out_ref,
               lng_ref, lnb_ref, out_ref, hst_ref, xs_ref, hb_ref, y_ref, xt_ref, ot_ref):
    ns = S5_SLAB_STATES
    lane_slabs = D_MODEL // LANES

    @pl.when(pl.program_id(0) == 0)
    def _():
        hst_ref[...] = jnp.zeros_like(hst_ref)

    half_steps = S5_STEPS // 2
    half_rows = half_steps * BATCH
    x_halves, z_halves = [], []
    for th in range(2):
        for b in range(BATCH):
            for ls in range(lane_slabs):
                col = b * D_MODEL + ls * LANES
                xt_ref[ls, pl.ds(th * half_rows + b, half_steps, stride=BATCH), :] = \
                    x_ref[th * half_steps:(th + 1) * half_steps, col:col + LANES]
        xh = jnp.concatenate([xt_ref[ls, th * half_rows:(th + 1) * half_rows, :] for ls in range(lane_slabs)],
                             axis=1)
        x_halves.append(xh)
        z_halves.append(_dot(xh.astype(BF16), w_in_ref[...]))
    z = jnp.concatenate(z_halves, axis=0)
    u = z[:, 0:D_MODEL]
    gate = z[:, D_MODEL:2 * D_MODEL]
    u_bf = u.astype(BF16)

    for s in range(S5_NSLAB):
        xs_ref[...] = _dot(u_bf[:, s * S5_SLAB:(s + 1) * S5_SLAB], bd_ref[s])
        lr = jnp.broadcast_to(lam_ref[s, 0:1, :], (BATCH, ns))
        li = jnp.broadcast_to(lam_ref[s, 1:2, :], (BATCH, ns))

        def step(t, h):
            hr, hi = h
            r = pl.multiple_of(t * BATCH, BATCH)
            xr = xs_ref[pl.ds(r, BATCH), 0:ns]
            xi = xs_ref[pl.ds(r, BATCH), ns:2 * ns]
            nhr = lr * hr - li * hi + xr
            nhi = lr * hi + li * hr + xi
            hb_ref[pl.ds(r, BATCH), 0:ns] = nhr.astype(BF16)
            hb_ref[pl.ds(r, BATCH), ns:2 * ns] = nhi.astype(BF16)
            return nhr, nhi

        hr, hi = lax.fori_loop(0, S5_STEPS, step, (hst_ref[s, 0], hst_ref[s, 1]), unroll=True)
        hst_ref[s, 0] = hr
        hst_ref[s, 1] = hi
        y_ref[:, s * S5_SLAB:(s + 1) * S5_SLAB] = _dot(hb_ref[...], cd_ref[s])

    y = y_ref[...] + d_ref[...] * u
    y = jax.nn.gelu(y)
    y = y * jax.nn.sigmoid(_dot(y.astype(BF16), w_glu_ref[...]) + b_glu_ref[...])
    y = y * jax.nn.silu(gate)
    y_bf = y.astype(BF16)
    for th in range(2):
        hrows = slice(th * half_rows, (th + 1) * half_rows)
        out = _dot(y_bf[hrows], w_out_ref[...])
        res = _residual_layer_norm(x_halves[th], out, lng_ref[...], lnb_ref[...])
        for ls in range(lane_slabs):
            ot_ref[ls, hrows, :] = res[:, ls * LANES:(ls + 1) * LANES]
        for b in range(BATCH):
            for ls in range(lane_slabs):
                col = b * D_MODEL + ls * LANES
                out_ref[th * half_steps:(th + 1) * half_steps, col:col + LANES] = \
                    ot_ref[ls, pl.ds(th * half_rows + b, half_steps, stride=BATCH), :]


def _s5_layer(x_sbd, w_in, a_re, a_im, log_dt, b_re, b_im, c_re, c_im, d_skip, w_glu, b_glu, w_out, ln_g, ln_b):
    gh = S5_GROUPS * S5_GROUP
    rep = lambda a: jnp.repeat(a, S5_GROUP, axis=0)
    small = jax.ShapeDtypeStruct((gh, S5_STATE), F32)
    abr, abi, bbr, bbi = pl.pallas_call(
        _s5_discretize_kernel,
        out_shape=[small] * 4,
        name="s5_discretize",
    )(rep(a_re), rep(a_im), rep(jnp.broadcast_to(log_dt[:, None], (S5_GROUPS, S5_STATE))),
      b_re.transpose(0, 2, 1).reshape(gh, S5_STATE), b_im.transpose(0, 2, 1).reshape(gh, S5_STATE))

    gl = S5_SLAB // S5_GROUP
    eye = jnp.eye(gl, dtype=F32)

    def block_diag_in(bb):
        t = bb.reshape(S5_NSLAB, gl, S5_GROUP, S5_STATE)
        return jnp.einsum('sghp,gk->sghkp', t, eye).reshape(S5_NSLAB, S5_SLAB, gl * S5_STATE)

    def block_diag_out(cc):
        t = cc.reshape(S5_NSLAB, gl, S5_GROUP, S5_STATE)
        return jnp.einsum('sghp,gk->skpgh', t, eye).reshape(S5_NSLAB, gl * S5_STATE, S5_SLAB)

    bd = jnp.concatenate([block_diag_in(bbr), block_diag_in(bbi)], axis=2).astype(BF16)
    cd = jnp.concatenate([block_diag_out(c_re), -block_diag_out(c_im)], axis=1).astype(BF16)
    lam = jnp.stack([abr[::S5_GROUP].reshape(S5_NSLAB, gl * S5_STATE),
                     abi[::S5_GROUP].reshape(S5_NSLAB, gl * S5_STATE)], axis=1)

    rows = S5_STEPS * BATCH
    ns2 = 2 * S5_SLAB_STATES
    row_spec = pl.BlockSpec((S5_STEPS, BATCH * D_MODEL), lambda i: (i, 0))
    return pl.pallas_call(
        _s5_kernel,
        grid=(SEQ // S5_STEPS,),
        in_specs=[
            row_spec,
            _const_spec((D_MODEL, 2 * D_MODEL)),
            _const_spec((S5_NSLAB, S5_SLAB, ns2)),
            _const_spec((S5_NSLAB, 2, S5_SLAB_STATES)),
            _const_spec((S5_NSLAB, ns2, S5_SLAB)),
            _const_spec((1, D_MODEL)),
            _const_spec((D_MODEL, D_MODEL)),
            _const_spec((1, D_MODEL)),
            _const_spec((D_MODEL, D_MODEL)),
            _const_spec((1, D_MODEL)),
            _const_spec((1, D_MODEL)),
        ],
        out_specs=row_spec,
        out_shape=jax.ShapeDtypeStruct((SEQ, BATCH * D_MODEL), F32),
        scratch_shapes=[
            pltpu.VMEM((S5_NSLAB, 2, BATCH, S5_SLAB_STATES), F32),
            pltpu.VMEM((rows, ns2), F32),
            pltpu.VMEM((rows, ns2), BF16),
            pltpu.VMEM((rows, D_MODEL), F32),
            pltpu.VMEM((D_MODEL // LANES, rows, LANES), F32),
            pltpu.VMEM((D_MODEL // LANES, rows, LANES), F32),
        ],
        compiler_params=pltpu.CompilerParams(
            dimension_semantics=("arbitrary",), vmem_limit_bytes=VMEM_LIMIT_BYTES),
        name="s5_layer",
    )(x_sbd, w_in.astype(BF16), bd, lam, cd, d_skip.reshape(1, D_MODEL), w_glu.astype(BF16),
      b_glu.reshape(1, D_MODEL), w_out.astype(BF16), ln_g.reshape(1, D_MODEL), ln_b.reshape(1, D_MODEL))


def kernel(x, hgrn_lower_bounds,
           l0_w_in, l0_norm_g, l0_w_out, l0_ln_g, l0_ln_b,
           l1_w_in, l1_w_out, l1_ln_g, l1_ln_b,
           l2_w_in, l2_a_re, l2_a_im, l2_log_dt, l2_b_re, l2_b_im, l2_c_re, l2_c_im,
           l2_d, l2_w_glu, l2_b_glu, l2_w_out, l2_ln_g, l2_ln_b,
           l3_w_in, l3_norm_g, l3_w_out, l3_ln_g, l3_ln_b):
    h = x.reshape(BATCH * SEQ, D_MODEL)
    h = _hgrn_layer(h, hgrn_lower_bounds, l0_w_in, l0_norm_g, l0_w_out, l0_ln_g, l0_ln_b,
                    layer=0, in_tb=False)
    h = _moba_layer(h, l1_w_in, l1_w_out, l1_ln_g, l1_ln_b)
    h = _s5_layer(h, l2_w_in, l2_a_re, l2_a_im, l2_log_dt, l2_b_re, l2_b_im,
                  l2_c_re, l2_c_im, l2_d, l2_w_glu, l2_b_glu, l2_w_out, l2_ln_g, l2_ln_b)
    h = _hgrn_layer(h, hgrn_lower_bounds, l3_w_in, l3_norm_g, l3_w_out,
                    l3_ln_g, l3_ln_b, layer=3, in_tb=True)
    return h.reshape(BATCH, SEQ, D_MODEL)
```

```python
import functools
import math

import numpy as np
import jax
import jax.numpy as jnp
from jax import lax
from jax.experimental import pallas as pl
from jax.experimental.pallas import tpu as pltpu

D_MODEL = 1024
BATCH = 16
SEQ = 2048
DEPTH = 4
HEADS = 8
HEAD_DIM = 128
MOBA_BLOCK = 256
MOBA_TOPK = 3
ROPE_THETA = 10000.0
S5_GROUP = 16
S5_GROUPS = 64
S5_STATE = 64
ALPHA = (2 * DEPTH) ** 0.25
LN_EPS = 1e-5
RMS_EPS = 1e-6
NEG = -1e30

VMEM_LIMIT_BYTES = 56 * 1024 * 1024
LANES = 128

HGRN_TILE = 256
HGRN_CHUNK = 128
PROJ_TILE = 512
S5_STEPS = 32
S5_SLAB = 256
S5_NSLAB = D_MODEL // S5_SLAB
S5_SLAB_STATES = (S5_SLAB // S5_GROUP) * S5_STATE

F32 = jnp.float32
BF16 = jnp.bfloat16


def _dot(a, b):
    return jnp.dot(a, b, preferred_element_type=F32)


def _dot_nt(a, b):
    return lax.dot_general(a, b, (((1,), (1,)), ((), ())), preferred_element_type=F32)


def _split2(a):
    hi = a.astype(BF16)
    lo = (a - hi.astype(F32)).astype(BF16)
    return hi, lo


def _block_diag(a, b):
    zero = jnp.zeros_like(a)
    return jnp.concatenate([jnp.concatenate([a, zero], axis=1), jnp.concatenate([zero, b], axis=1)], axis=0)


def _residual_layer_norm(x, y, g, b):
    r = ALPHA * x + y
    mu = jnp.mean(r, axis=-1, keepdims=True)
    c = r - mu
    var = jnp.mean(c * c, axis=-1, keepdims=True)
    return c * lax.rsqrt(var + LN_EPS) * g + b


def _const_spec(shape):
    nd = len(shape)
    return pl.BlockSpec(shape, lambda *_: (0,) * nd, pipeline_mode=pl.Buffered(1))


def _hgrn_level_table(c):
    t = np.arange(c)[:, None]
    s = np.arange(c)[None, :]
    x = np.bitwise_xor(t, s)
    lv = np.zeros((c, c), np.int32)
    nz = x > 0
    lv[nz] = 2 ** (np.floor(np.log2(x[nz])).astype(np.int32) + 1)
    lv[s >= t] = 0
    return lv


def _segment_mid_rows(bc, seg):
    c, w = bc.shape
    half = seg // 2
    row = lax.broadcasted_iota(jnp.int32, (8, w), 0)
    pieces = []
    for a in range(0, c, 8):
        cand = [jnp.broadcast_to(bc[a + o + half - 1:a + o + half, :], (8, w)) for o in range(0, 8, seg)]
        out = cand[-1]
        for idx in range(len(cand) - 2, -1, -1):
            out = jnp.where(row < (idx + 1) * seg, cand[idx], out)
        pieces.append(out)
    return jnp.concatenate(pieces, axis=0)


def _level_operand(q, k, bc, seg, second):
    c, _ = bc.shape
    half = seg // 2
    if half >= 8:
        pieces = []
        for a in range(0, c, seg):
            m = bc[a + half - 1:a + half, :]
            pieces.append(k[a:a + half] * jnp.exp2(m - bc[a:a + half]))
            pieces.append(q[a + half:a + seg] * jnp.exp2(bc[a + half:a + seg] - m))
        return jnp.concatenate(pieces, axis=0)
    decay = jnp.exp2(-jnp.abs(bc - _segment_mid_rows(bc, seg)))
    return jnp.where(second, q, k) * decay


HGRN_PROJ_PIECE = 256


def _hgrn_kernel(layer, x_ref, xn_ref, lbl_ref, lv_ref, tri_ref, w_in_ref, ng_ref, w_out_ref, lng_ref, lnb_ref,
                 out_ref, st_ref, o_ref, xb_ref, za_ref, zb_ref):
    t = HGRN_TILE

    def proj_pieces(src, slot, z_ref):
        def cast():
            xb_ref[slot] = src[...].astype(BF16)

        def piece(j):
            cols = slice(j * HGRN_PROJ_PIECE, (j + 1) * HGRN_PROJ_PIECE)
            z_ref[:, cols] = _dot(xb_ref[slot], w_in_ref[:, cols])

        return [cast] + [functools.partial(piece, j) for j in range(4 * D_MODEL // HGRN_PROJ_PIECE)]

    @pl.when((pl.program_id(0) == 0) & (pl.program_id(1) == 0))
    def _():
        for job in proj_pieces(x_ref.at[0:t, :], 0, za_ref):
            job()

    @pl.when(pl.program_id(1) == 0)
    def _():
        st_ref[...] = jnp.zeros_like(st_ref)

    lbl = lbl_ref[...]
    e = jnp.exp(lbl - jnp.max(lbl, axis=0, keepdims=True))
    sm = e / jnp.sum(e, axis=0, keepdims=True)
    lb = jnp.zeros((1, D_MODEL), F32)
    for r in range(1, layer + 1):
        lb = lb + sm[r:r + 1, :]

    mix = functools.partial(_hgrn_mixer, lb, lv_ref, tri_ref, ng_ref, w_out_ref, lng_ref, lnb_ref, st_ref)
    out_ref[0:t, :] = mix(za_ref, x_ref.at[0:t, :], o_ref.at[0], proj_pieces(x_ref.at[t:2 * t, :], 0, zb_ref))
    out_ref[t:2 * t, :] = mix(zb_ref, x_ref.at[t:2 * t, :], o_ref.at[1], proj_pieces(xn_ref, 1, za_ref))


def _hgrn_mixer(lb, lv_ref, tri_ref, ng_ref, w_out_ref, lng_ref, lnb_ref, st_ref, z_ref, x_ref, o_ref, side_jobs):
    c = HGRN_CHUNK
    side_jobs = list(side_jobs)

    def run_side_job():
        if side_jobs:
            side_jobs.pop(0)()

    lv2 = lv_ref[...]
    tri2 = tri_ref[...]
    ng = ng_ref[...]
    row_in_chunk = lax.broadcasted_iota(jnp.int32, (c, D_MODEL), 0)

    for ci in range(HGRN_TILE // c):
        rows = slice(ci * c, (ci + 1) * c)
        q = jax.nn.silu(z_ref[rows, 0:D_MODEL])
        f = lb + (1.0 - lb) * jax.nn.sigmoid(z_ref[rows, D_MODEL:2 * D_MODEL])
        g = jnp.log2(f)
        k = 1.0 - f
        v = z_ref[rows, 2 * D_MODEL:3 * D_MODEL]
        gate = jax.nn.silu(z_ref[rows, 3 * D_MODEL:4 * D_MODEL])
        run_side_job()
        bc = _dot(tri2, jnp.concatenate(_split2(g), axis=0))
        run_side_job()

        w_levels = []
        seg = c
        while seg > 2:
            second = (row_in_chunk & (seg - 1)) >= (seg // 2)
            w_levels.append((seg, _level_operand(q, k, bc, seg, second).astype(BF16)))
            seg //= 2
        odd = (row_in_chunk & 1) == 1
        w_levels.append((2, jnp.where(odd, q * f, k).astype(BF16)))

        qd = (q * jnp.exp2(bc)).astype(BF16)
        b_last = bc[c - 1:c, :]
        kd = (k * jnp.exp2(b_last - bc)).astype(BF16)
        d_last = jnp.exp2(b_last)
        qk = q * k
        run_side_job()

        for hp in range(HEADS // 2):
            pair = slice(2 * hp * HEAD_DIM, 2 * (hp + 1) * HEAD_DIM)
            scores2 = jnp.zeros((c, 2 * c), F32)
            for seg, w in w_levels:
                w2 = w[:, pair]
                scores2 = jnp.where(lv2 == seg, _dot_nt(w2, _block_diag(w2[:, :HEAD_DIM], w2[:, HEAD_DIM:])), scores2)
            st2 = st_ref[:, pair]
            vt = [v[:, pair][:, i * HEAD_DIM:(i + 1) * HEAD_DIM].T.astype(BF16) for i in range(2)]
            kd2 = kd[:, pair]
            upd = _dot(jnp.concatenate(vt, axis=1), _block_diag(kd2[:, :HEAD_DIM], kd2[:, HEAD_DIM:]))
            st_ref[:, pair] = st2 * d_last[:, pair] + upd
            for i in range(2):
                sl = slice((2 * hp + i) * HEAD_DIM, (2 * hp + i + 1) * HEAD_DIM)
                lhs = jnp.concatenate([scores2[:, i * c:(i + 1) * c].astype(BF16), qd[:, sl]], axis=1)
                rhs_t = jnp.concatenate([vt[i], st2[:, i * HEAD_DIM:(i + 1) * HEAD_DIM].astype(BF16)], axis=1)
                o = _dot_nt(lhs, rhs_t) + jnp.sum(qk[:, sl], axis=-1, keepdims=True) * v[:, sl]
                o = o * lax.rsqrt(jnp.mean(o * o, axis=-1, keepdims=True) + RMS_EPS) * ng[:, sl]
                o_ref[rows, sl] = (o * gate[:, sl]).astype(BF16)
                run_side_job()

    while side_jobs:
        run_side_job()
    y = _dot(o_ref[...], w_out_ref[...])
    return _residual_layer_norm(x_ref[...], y, lng_ref[...], lnb_ref[...])


def _hgrn_layer(x2d, lb_logits, w_in, norm_g, w_out, ln_g, ln_b, *, layer, in_tb, out_tb):
    nt = SEQ // HGRN_TILE
    ns = nt // 2
    pair = (2 * HGRN_TILE, D_MODEL)
    tile = (HGRN_TILE, D_MODEL)
    tb_spec = pl.BlockSpec(pair, lambda b, c: (c, b))
    bt_spec = pl.BlockSpec(pair, lambda b, c: (b * ns + c, 0))

    def next_tile(b, c):
        return jnp.minimum(2 * (b * ns + c) + 2, BATCH * nt - 1)

    tb_next = pl.BlockSpec(tile, lambda b, c: (next_tile(b, c) % nt, next_tile(b, c) // nt))
    bt_next = pl.BlockSpec(tile, lambda b, c: (next_tile(b, c), 0))
    out_shape = (SEQ, BATCH * D_MODEL) if out_tb else (BATCH * SEQ, D_MODEL)
    lv = _hgrn_level_table(HGRN_CHUNK)
    lv2 = jnp.asarray(np.concatenate([lv, lv], axis=1))
    tri = np.tril(np.ones((HGRN_CHUNK, HGRN_CHUNK), np.float32))
    tri2 = jnp.asarray(np.concatenate([tri, tri], axis=1), dtype=BF16)
    return pl.pallas_call(
        functools.partial(_hgrn_kernel, layer),
        grid=(BATCH, ns),
        in_specs=[
            tb_spec if in_tb else bt_spec,
            tb_next if in_tb else bt_next,
            _const_spec((DEPTH, D_MODEL)),
            _const_spec((HGRN_CHUNK, 2 * HGRN_CHUNK)),
            _const_spec((HGRN_CHUNK, 2 * HGRN_CHUNK)),
            _const_spec((D_MODEL, 4 * D_MODEL)),
            _const_spec((1, D_MODEL)),
            _const_spec((D_MODEL, D_MODEL)),
            _const_spec((1, D_MODEL)),
            _const_spec((1, D_MODEL)),
        ],
        out_specs=tb_spec if out_tb else bt_spec,
        out_shape=jax.ShapeDtypeStruct(out_shape, F32),
        scratch_shapes=[
            pltpu.VMEM((HEAD_DIM, D_MODEL), F32),
            pltpu.VMEM((2, HGRN_TILE, D_MODEL), BF16),
            pltpu.VMEM((2, HGRN_TILE, D_MODEL), BF16),
            pltpu.VMEM((HGRN_TILE, 4 * D_MODEL), F32),
            pltpu.VMEM((HGRN_TILE, 4 * D_MODEL), F32),
        ],
        compiler_params=pltpu.CompilerParams(
            dimension_semantics=("arbitrary", "arbitrary"),
            vmem_limit_bytes=VMEM_LIMIT_BYTES),
        name=f"hgrn2_layer{layer}",
    )(x2d, x2d, lb_logits, lv2, tri2, w_in.astype(BF16), norm_g.reshape(1, D_MODEL), w_out.astype(BF16),
      ln_g.reshape(1, D_MODEL), ln_b.reshape(1, D_MODEL))


def _moba_proj_kernel(x_ref, w_in_ref, cq_ref, sq_ref, ck_ref, sk_ref, q_ref, k_ref, vt_ref, gate_ref):
    z = _dot(x_ref[...].astype(BF16), w_in_ref[...])
    cq, sq, ck, sk = cq_ref[...], sq_ref[...], ck_ref[...], sk_ref[...]
    for h in range(HEADS):
        qh = z[:, h * HEAD_DIM:(h + 1) * HEAD_DIM]
        kh = z[:, D_MODEL + h * HEAD_DIM:D_MODEL + (h + 1) * HEAD_DIM]
        vh = z[:, 2 * D_MODEL + h * HEAD_DIM:2 * D_MODEL + (h + 1) * HEAD_DIM]
        q_ref[0, h] = (qh * cq + pltpu.roll(qh, HEAD_DIM // 2, 1) * sq).astype(BF16)
        k_ref[0, h] = (kh * ck + pltpu.roll(kh, HEAD_DIM // 2, 1) * sk).astype(BF16)
        vt_ref[0, h] = vh.T.astype(BF16)
    gate_ref[...] = jax.nn.silu(z[:, 3 * D_MODEL:4 * D_MODEL]).astype(BF16)


def _moba_attn_kernel(q_ref, k_ref, vt_ref, gate_ref, x_ref, w_out_ref, lng_ref, lnb_ref,
                      out_ref, km_ref, o_ref, s_ref):
    n = pl.program_id(1)
    blk = MOBA_BLOCK
    nb = SEQ // blk

    @pl.when(n == 0)
    def _():
        for h in range(HEADS):
            kf = k_ref[0, h].astype(F32).reshape(nb, blk, HEAD_DIM)
            km_ref[h] = jnp.mean(kf, axis=1)

    key = lax.broadcasted_iota(jnp.int32, (blk, blk), 0)
    qry = lax.broadcasted_iota(jnp.int32, (blk, blk), 1)
    causal = key <= qry

    def attend(n_past):
        blocks = [slice(j * blk, (j + 1) * blk) for j in range(n_past + 1)]

        def selection(h, q):
            if n_past <= MOBA_TOPK:
                return None
            km = km_ref[h]
            km_hi = km.astype(BF16)
            km_lo = (km - km_hi.astype(F32)).astype(BF16)
            gsc = _dot_nt(km_hi, q) + _dot_nt(km_lo, q)
            g = [gsc[j:j + 1, :] for j in range(n_past)]
            sel = []
            for j in range(n_past):
                cnt = jnp.zeros((1, blk), F32)
                for i in range(n_past):
                    if i != j:
                        beats = (g[i] >= g[j]) if i < j else (g[i] > g[j])
                        cnt = cnt + beats.astype(F32)
                sel.append(cnt < MOBA_TOPK)
            return sel

        def score_block(h, q, sel, j, slot):
            s = _dot_nt(k_ref[0, h, blocks[j], :], q)
            if j == n_past:
                s = jnp.where(causal, s, NEG)
            elif sel is not None:
                s = jnp.where(sel[j], s, NEG)
            s_ref[slot, blocks[j], :] = s
            return jnp.max(s, axis=0, keepdims=True)

        def value_block(h, m, j, slot):
            p = jnp.exp2(s_ref[slot, blocks[j], :] - m)
            return _dot(vt_ref[0, h, :, blocks[j]], p.astype(BF16)), jnp.sum(p, axis=0, keepdims=True)

        def stage(h_cur, m_cur, h_next):
            if h_next is not None:
                q = q_ref[0, h_next]
                sel = selection(h_next, q)
            m_next = acc = l = None
            for j in range(n_past + 1):
                if h_next is not None:
                    mj = score_block(h_next, q, sel, j, h_next % 2)
                    m_next = mj if m_next is None else jnp.maximum(m_next, mj)
                if h_cur is not None:
                    aj, lj = value_block(h_cur, m_cur, j, h_cur % 2)
                    acc = aj if acc is None else acc + aj
                    l = lj if l is None else l + lj
            if h_cur is not None:
                o_ref[h_cur] = (acc * (1.0 / l)).T
            return m_next

        m = stage(None, None, 0)
        for h in range(HEADS):
            m = stage(h, m, h + 1 if h + 1 < HEADS else None)

    for n_past in range(nb):
        pl.when(n == n_past)(functools.partial(attend, n_past))

    o = jnp.concatenate([o_ref[h] for h in range(HEADS)], axis=1)
    o = (o * gate_ref[...].astype(F32)).astype(BF16)
    y = _dot(o, w_out_ref[...])
    out_ref[...] = _residual_layer_norm(x_ref[...], y, lng_ref[...], lnb_ref[...])


def _rope_tables():
    pos = np.arange(SEQ, dtype=np.float32)
    inv_freq = (1.0 / (np.float32(ROPE_THETA) ** (np.arange(0, HEAD_DIM, 2, dtype=np.float32) / HEAD_DIM))
                ).astype(np.float32)
    ang = pos[:, None] * inv_freq[None, :]
    cos, sin = np.cos(ang).astype(np.float32), np.sin(ang).astype(np.float32)
    c = np.concatenate([cos, cos], axis=1)
    s = np.concatenate([-sin, sin], axis=1)
    scale = np.float32(HEAD_DIM ** -0.5 * math.log2(math.e))
    return c * scale, s * scale, c, s


def _moba_layer(x2d, w_in, w_out, ln_g, ln_b):
    tiles_per_seq = SEQ // PROJ_TILE
    hshape = (BATCH, HEADS, SEQ, HEAD_DIM)
    hspec = pl.BlockSpec((1, HEADS, PROJ_TILE, HEAD_DIM), lambda i: (i // tiles_per_seq, 0, i % tiles_per_seq, 0))
    tspec = pl.BlockSpec((PROJ_TILE, HEAD_DIM), lambda i: (i % tiles_per_seq, 0))
    nb = SEQ // MOBA_BLOCK
    vt_shape = (BATCH, HEADS, HEAD_DIM, SEQ)
    vt_spec = pl.BlockSpec((1, HEADS, HEAD_DIM, PROJ_TILE),
                           lambda i: (i // tiles_per_seq, 0, 0, i % tiles_per_seq))
    cq, sq, ck, sk = (jnp.asarray(t) for t in _rope_tables())
    q, k, vt, gate = pl.pallas_call(
        _moba_proj_kernel,
        grid=(BATCH * tiles_per_seq,),
        in_specs=[
            pl.BlockSpec((PROJ_TILE, D_MODEL), lambda i: (i, 0)),
            _const_spec((D_MODEL, 4 * D_MODEL)),
            tspec, tspec, tspec, tspec,
        ],
        out_specs=[hspec, hspec, vt_spec, pl.BlockSpec((PROJ_TILE, D_MODEL), lambda i: (i, 0))],
        out_shape=[jax.ShapeDtypeStruct(hshape, BF16)] * 2 + [jax.ShapeDtypeStruct(vt_shape, BF16),
                                                              jax.ShapeDtypeStruct((BATCH * SEQ, D_MODEL), BF16)],
        compiler_params=pltpu.CompilerParams(
            dimension_semantics=("arbitrary",), vmem_limit_bytes=VMEM_LIMIT_BYTES),
        name="moba_proj",
    )(x2d, w_in.astype(BF16), cq, sq, ck, sk)

    k_spec = pl.BlockSpec((1, HEADS, SEQ, HEAD_DIM), lambda b, n: (b, 0, 0, 0))
    vt_in_spec = pl.BlockSpec((1, HEADS, HEAD_DIM, SEQ), lambda b, n: (b, 0, 0, 0))
    row_spec = pl.BlockSpec((MOBA_BLOCK, D_MODEL), lambda b, n: (b * nb + n, 0))
    return pl.pallas_call(
        _moba_attn_kernel,
        grid=(BATCH, nb),
        in_specs=[
            pl.BlockSpec((1, HEADS, MOBA_BLOCK, HEAD_DIM), lambda b, n: (b, 0, n, 0)),
            k_spec, vt_in_spec, row_spec, row_spec,
            _const_spec((D_MODEL, D_MODEL)),
            _const_spec((1, D_MODEL)),
            _const_spec((1, D_MODEL)),
        ],
        out_specs=pl.BlockSpec((MOBA_BLOCK, D_MODEL), lambda b, n: (n, b)),
        out_shape=jax.ShapeDtypeStruct((SEQ, BATCH * D_MODEL), F32),
        scratch_shapes=[
            pltpu.VMEM((HEADS, nb, HEAD_DIM), F32),
            pltpu.VMEM((HEADS, MOBA_BLOCK, HEAD_DIM), F32),
            pltpu.VMEM((2, SEQ, MOBA_BLOCK), F32),
        ],
        compiler_params=pltpu.CompilerParams(
            dimension_semantics=("arbitrary", "arbitrary"), vmem_limit_bytes=VMEM_LIMIT_BYTES),
        name="moba_attn",
    )(q, k, vt, gate, x2d, w_out.astype(BF16), ln_g.reshape(1, D_MODEL), ln_b.reshape(1, D_MODEL))


def _s5_discretize_kernel(ar_ref, ai_ref, ldt_ref, br_ref, bi_ref, abr_ref, abi_ref, bbr_ref, bbi_ref):
    ar, ai = ar_ref[...], ai_ref[...]
    dt = jnp.exp(ldt_ref[...])
    mag = jnp.exp(dt * ar)
    abr = mag * jnp.cos(dt * ai)
    abi = mag * jnp.sin(dt * ai)
    nr, ni = abr - 1.0, abi
    den = ar * ar + ai * ai
    zr = (nr * ar + ni * ai) / den
    zi = (ni * ar - nr * ai) / den
    br, bi = br_ref[...], bi_ref[...]
    abr_ref[...] = abr
    abi_ref[...] = abi
    bbr_ref[...] = zr * br - zi * bi
    bbi_ref[...] = zr * bi + zi * br


def _s5_kernel(x_hbm_ref, w_in_ref, bd_ref, lam_ref, cd_ref, d_ref, w_glu_ref, b_glu_ref, w_out_ref,
               lng_ref, lnb_ref, out_ref, hst_ref, xs_ref, hb_ref, y_ref, xt_ref, ot_ref, in_sem):
    ns = S5_SLAB_STATES
    lane_slabs = D_MODEL // LANES
    i = pl.program_id(0)
    slot = i % 2

    def tile_copies(tile, dst_slot):
        return [pltpu.make_async_copy(
            x_hbm_ref.at[pl.ds(tile * S5_STEPS, S5_STEPS), pl.ds(b * D_MODEL, D_MODEL)],
            xt_ref.at[dst_slot, :, b, :], in_sem.at[dst_slot]) for b in range(BATCH)]

    @pl.when(i == 0)
    def _():
        hst_ref[...] = jnp.zeros_like(hst_ref)
        for cp in tile_copies(0, 0):
            cp.start()

    @pl.when(i + 1 < pl.num_programs(0))
    def _():
        for cp in tile_copies(i + 1, 1 - slot):
            cp.start()

    for cp in tile_copies(i, slot):
        cp.wait()

    half_steps = S5_STEPS // 2
    half_rows = half_steps * BATCH
    x_tile = xt_ref[slot].reshape(S5_STEPS * BATCH, D_MODEL)
    x_halves, z_halves = [], []
    for th in range(2):
        xh = x_tile[th * half_rows:(th + 1) * half_rows]
        x_halves.append(xh)
        z_halves.append(_dot(xh.astype(BF16), w_in_ref[...]))
    z = jnp.concatenate(z_halves, axis=0)
    u = z[:, 0:D_MODEL]
    gate = z[:, D_MODEL:2 * D_MODEL]
    u_bf = u.astype(BF16)

    for s in range(S5_NSLAB):
        xs_ref[...] = _dot(u_bf[:, s * S5_SLAB:(s + 1) * S5_SLAB], bd_ref[s])
        lr = jnp.broadcast_to(lam_ref[s, 0:1, :], (BATCH, ns))
        li = jnp.broadcast_to(lam_ref[s, 1:2, :], (BATCH, ns))

        def step(t, h):
            hr, hi = h
            r = pl.multiple_of(t * BATCH, BATCH)
            xr = xs_ref[pl.ds(r, BATCH), 0:ns]
            xi = xs_ref[pl.ds(r, BATCH), ns:2 * ns]
            nhr = lr * hr - li * hi + xr
            nhi = lr * hi + li * hr + xi
            hb_ref[pl.ds(r, BATCH), 0:ns] = nhr.astype(BF16)
            hb_ref[pl.ds(r, BATCH), ns:2 * ns] = nhi.astype(BF16)
            return nhr, nhi

        hr, hi = lax.fori_loop(0, S5_STEPS, step, (hst_ref[s, 0], hst_ref[s, 1]), unroll=True)
        hst_ref[s, 0] = hr
        hst_ref[s, 1] = hi
        y_ref[:, s * S5_SLAB:(s + 1) * S5_SLAB] = _dot(hb_ref[...], cd_ref[s])

    y = y_ref[...] + d_ref[...] * u
    y = jax.nn.gelu(y)
    y = y * jax.nn.sigmoid(_dot(y.astype(BF16), w_glu_ref[...]) + b_glu_ref[...])
    y = y * jax.nn.silu(gate)
    y_bf = y.astype(BF16)
    for th in range(2):
        hrows = slice(th * half_rows, (th + 1) * half_rows)
        out = _dot(y_bf[hrows], w_out_ref[...])
        res = _residual_layer_norm(x_halves[th], out, lng_ref[...], lnb_ref[...])
        for ls in range(lane_slabs):
            ot_ref[ls, hrows, :] = res[:, ls * LANES:(ls + 1) * LANES]
        for b in range(BATCH):
            for ls in range(lane_slabs):
                col = b * D_MODEL + ls * LANES
                out_ref[th * half_steps:(th + 1) * half_steps, col:col + LANES] = \
                    ot_ref[ls, pl.ds(th * half_rows + b, half_steps, stride=BATCH), :]


def _s5_layer(x_sbd, w_in, a_re, a_im, log_dt, b_re, b_im, c_re, c_im, d_skip, w_glu, b_glu, w_out, ln_g, ln_b):
    gh = S5_GROUPS * S5_GROUP
    rep = lambda a: jnp.repeat(a, S5_GROUP, axis=0)
    small = jax.ShapeDtypeStruct((gh, S5_STATE), F32)
    abr, abi, bbr, bbi = pl.pallas_call(
        _s5_discretize_kernel,
        out_shape=[small] * 4,
        name="s5_discretize",
    )(rep(a_re), rep(a_im), rep(jnp.broadcast_to(log_dt[:, None], (S5_GROUPS, S5_STATE))),
      b_re.transpose(0, 2, 1).reshape(gh, S5_STATE), b_im.transpose(0, 2, 1).reshape(gh, S5_STATE))

    gl = S5_SLAB // S5_GROUP
    eye = jnp.eye(gl, dtype=F32)

    def block_diag_in(bb):
        t = bb.reshape(S5_NSLAB, gl, S5_GROUP, S5_STATE)
        return jnp.einsum('sghp,gk->sghkp', t, eye).reshape(S5_NSLAB, S5_SLAB, gl * S5_STATE)

    def block_diag_out(cc):
        t = cc.reshape(S5_NSLAB, gl, S5_GROUP, S5_STATE)
        return jnp.einsum('sghp,gk->skpgh', t, eye).reshape(S5_NSLAB, gl * S5_STATE, S5_SLAB)

    bd = jnp.concatenate([block_diag_in(bbr), block_diag_in(bbi)], axis=2).astype(BF16)
    cd = jnp.concatenate([block_diag_out(c_re), -block_diag_out(c_im)], axis=1).astype(BF16)
    lam = jnp.stack([abr[::S5_GROUP].reshape(S5_NSLAB, gl * S5_STATE),
                     abi[::S5_GROUP].reshape(S5_NSLAB, gl * S5_STATE)], axis=1)

    rows = S5_STEPS * BATCH
    ns2 = 2 * S5_SLAB_STATES
    row_spec = pl.BlockSpec((S5_STEPS, BATCH * D_MODEL), lambda i: (i, 0))
    return pl.pallas_call(
        _s5_kernel,
        grid=(SEQ // S5_STEPS,),
        in_specs=[
            pl.BlockSpec(memory_space=pl.ANY),
            _const_spec((D_MODEL, 2 * D_MODEL)),
            _const_spec((S5_NSLAB, S5_SLAB, ns2)),
            _const_spec((S5_NSLAB, 2, S5_SLAB_STATES)),
            _const_spec((S5_NSLAB, ns2, S5_SLAB)),
            _const_spec((1, D_MODEL)),
            _const_spec((D_MODEL, D_MODEL)),
            _const_spec((1, D_MODEL)),
            _const_spec((D_MODEL, D_MODEL)),
            _const_spec((1, D_MODEL)),
            _const_spec((1, D_MODEL)),
        ],
        out_specs=row_spec,
        out_shape=jax.ShapeDtypeStruct((SEQ, BATCH * D_MODEL), F32),
        scratch_shapes=[
            pltpu.VMEM((S5_NSLAB, 2, BATCH, S5_SLAB_STATES), F32),
            pltpu.VMEM((rows, ns2), F32),
            pltpu.VMEM((rows, ns2), BF16),
            pltpu.VMEM((rows, D_MODEL), F32),
            pltpu.VMEM((2, S5_STEPS, BATCH, D_MODEL), F32),
            pltpu.VMEM((D_MODEL // LANES, rows, LANES), F32),
            pltpu.SemaphoreType.DMA((2,)),
        ],
        compiler_params=pltpu.CompilerParams(
            dimension_semantics=("arbitrary",), vmem_limit_bytes=VMEM_LIMIT_BYTES),
        name="s5_layer",
    )(x_sbd, w_in.astype(BF16), bd, lam, cd, d_skip.reshape(1, D_MODEL), w_glu.astype(BF16),
      b_glu.reshape(1, D_MODEL), w_out.astype(BF16), ln_g.reshape(1, D_MODEL), ln_b.reshape(1, D_MODEL))


def kernel(x, hgrn_lower_bounds,
           l0_w_in, l0_norm_g, l0_w_out, l0_ln_g, l0_ln_b,
           l1_w_in, l1_w_out, l1_ln_g, l1_ln_b,
           l2_w_in, l2_a_re, l2_a_im, l2_log_dt, l2_b_re, l2_b_im, l2_c_re, l2_c_im,
           l2_d, l2_w_glu, l2_b_glu, l2_w_out, l2_ln_g, l2_ln_b,
           l3_w_in, l3_norm_g, l3_w_out, l3_ln_g, l3_ln_b):
    h = x.reshape(BATCH * SEQ, D_MODEL)
    h = _hgrn_layer(h, hgrn_lower_bounds, l0_w_in, l0_norm_g, l0_w_out, l0_ln_g, l0_ln_b,
                    layer=0, in_tb=False, out_tb=False)
    h = _moba_layer(h, l1_w_in, l1_w_out, l1_ln_g, l1_ln_b)
    h = _s5_layer(h, l2_w_in, l2_a_re, l2_a_im, l2_log_dt, l2_b_re, l2_b_im,
                  l2_c_re, l2_c_im, l2_d, l2_w_glu, l2_b_glu, l2_w_out, l2_ln_g, l2_ln_b)
    h = _hgrn_layer(h, hgrn_lower_bounds, l3_w_in, l3_norm_g, l3_w_out,
                    l3_ln_g, l3_ln_b, layer=3, in_tb=True, out_tb=False)
    return h.reshape(BATCH, SEQ, D_MODEL)
```

```python
import functools
import math

import numpy as np
import jax
import jax.numpy as jnp
from jax import lax
from jax.experimental import pallas as pl
from jax.experimental.pallas import tpu as pltpu

D_MODEL = 1024
BATCH = 16
SEQ = 2048
DEPTH = 4
HEADS = 8
HEAD_DIM = 128
MOBA_BLOCK = 256
MOBA_TOPK = 3
ROPE_THETA = 10000.0
S5_GROUP = 16
S5_GROUPS = 64
S5_STATE = 64
ALPHA = (2 * DEPTH) ** 0.25
LN_EPS = 1e-5
RMS_EPS = 1e-6
NEG = -1e30

VMEM_LIMIT_BYTES = 56 * 1024 * 1024
LANES = 128

HGRN_TILE = 256
HGRN_CHUNK = 128
PROJ_TILE = 512
S5_STEPS = 32
S5_SLAB = 256
S5_NSLAB = D_MODEL // S5_SLAB
S5_SLAB_STATES = (S5_SLAB // S5_GROUP) * S5_STATE

F32 = jnp.float32
BF16 = jnp.bfloat16


def _dot(a, b):
    return jnp.dot(a, b, preferred_element_type=F32)


def _dot_nt(a, b):
    return lax.dot_general(a, b, (((1,), (1,)), ((), ())), preferred_element_type=F32)


def _split2(a):
    hi = a.astype(BF16)
    lo = (a - hi.astype(F32)).astype(BF16)
    return hi, lo


def _block_diag(a, b):
    zero = jnp.zeros_like(a)
    return jnp.concatenate([jnp.concatenate([a, zero], axis=1), jnp.concatenate([zero, b], axis=1)], axis=0)


def _residual_layer_norm(x, y, g, b):
    r = ALPHA * x + y
    mu = jnp.mean(r, axis=-1, keepdims=True)
    c = r - mu
    var = jnp.mean(c * c, axis=-1, keepdims=True)
    return c * lax.rsqrt(var + LN_EPS) * g + b


def _const_spec(shape):
    nd = len(shape)
    return pl.BlockSpec(shape, lambda *_: (0,) * nd, pipeline_mode=pl.Buffered(1))


def _hgrn_level_table(c):
    t = np.arange(c)[:, None]
    s = np.arange(c)[None, :]
    x = np.bitwise_xor(t, s)
    lv = np.zeros((c, c), np.int32)
    nz = x > 0
    lv[nz] = 2 ** (np.floor(np.log2(x[nz])).astype(np.int32) + 1)
    lv[s >= t] = 0
    return lv


def _segment_mid_rows(bc, seg):
    c, w = bc.shape
    half = seg // 2
    row = lax.broadcasted_iota(jnp.int32, (8, w), 0)
    pieces = []
    for a in range(0, c, 8):
        cand = [jnp.broadcast_to(bc[a + o + half - 1:a + o + half, :], (8, w)) for o in range(0, 8, seg)]
        out = cand[-1]
        for idx in range(len(cand) - 2, -1, -1):
            out = jnp.where(row < (idx + 1) * seg, cand[idx], out)
        pieces.append(out)
    return jnp.concatenate(pieces, axis=0)


def _level_operand(q, k, bc, seg, second):
    c, _ = bc.shape
    half = seg // 2
    if half >= 8:
        pieces = []
        for a in range(0, c, seg):
            m = bc[a + half - 1:a + half, :]
            pieces.append(k[a:a + half] * jnp.exp2(m - bc[a:a + half]))
            pieces.append(q[a + half:a + seg] * jnp.exp2(bc[a + half:a + seg] - m))
        return jnp.concatenate(pieces, axis=0)
    decay = jnp.exp2(-jnp.abs(bc - _segment_mid_rows(bc, seg)))
    return jnp.where(second, q, k) * decay


HGRN_PROJ_PIECE = 256


def _hgrn_kernel(layer, x_ref, xn_ref, lbl_ref, lv_ref, tri_ref, w_in_ref, ng_ref, w_out_ref, lng_ref, lnb_ref,
                 out_ref, st_ref, o_ref, xb_ref, za_ref, zb_ref):
    t = HGRN_TILE

    def proj_pieces(src, slot, z_ref):
        def cast():
            xb_ref[slot] = src[...].astype(BF16)

        def piece(j):
            cols = slice(j * HGRN_PROJ_PIECE, (j + 1) * HGRN_PROJ_PIECE)
            z_ref[:, cols] = _dot(xb_ref[slot], w_in_ref[:, cols])

        return [cast] + [functools.partial(piece, j) for j in range(4 * D_MODEL // HGRN_PROJ_PIECE)]

    @pl.when((pl.program_id(0) == 0) & (pl.program_id(1) == 0))
    def _():
        for job in proj_pieces(x_ref.at[0:t, :], 0, za_ref):
            job()

    @pl.when(pl.program_id(1) == 0)
    def _():
        st_ref[...] = jnp.zeros_like(st_ref)

    lbl = lbl_ref[...]
    e = jnp.exp(lbl - jnp.max(lbl, axis=0, keepdims=True))
    sm = e / jnp.sum(e, axis=0, keepdims=True)
    lb = jnp.zeros((1, D_MODEL), F32)
    for r in range(1, layer + 1):
        lb = lb + sm[r:r + 1, :]

    mix = functools.partial(_hgrn_mixer, lb, lv_ref, tri_ref, ng_ref, w_out_ref, lng_ref, lnb_ref, st_ref)
    out_ref[0:t, :] = mix(za_ref, x_ref.at[0:t, :], o_ref.at[0], proj_pieces(x_ref.at[t:2 * t, :], 0, zb_ref))
    out_ref[t:2 * t, :] = mix(zb_ref, x_ref.at[t:2 * t, :], o_ref.at[1], proj_pieces(xn_ref, 1, za_ref))


def _hgrn_mixer(lb, lv_ref, tri_ref, ng_ref, w_out_ref, lng_ref, lnb_ref, st_ref, z_ref, x_ref, o_ref, side_jobs):
    c = HGRN_CHUNK
    side_jobs = list(side_jobs)

    def run_side_job():
        if side_jobs:
            side_jobs.pop(0)()

    lv2 = lv_ref[...]
    tri2 = tri_ref[...]
    ng = ng_ref[...]
    row_in_chunk = lax.broadcasted_iota(jnp.int32, (c, D_MODEL), 0)

    for ci in range(HGRN_TILE // c):
        rows = slice(ci * c, (ci + 1) * c)
        q = jax.nn.silu(z_ref[rows, 0:D_MODEL])
        f = lb + (1.0 - lb) * jax.nn.sigmoid(z_ref[rows, D_MODEL:2 * D_MODEL])
        g = jnp.log2(f)
        k = 1.0 - f
        v = z_ref[rows, 2 * D_MODEL:3 * D_MODEL]
        gate = jax.nn.silu(z_ref[rows, 3 * D_MODEL:4 * D_MODEL])
        run_side_job()
        bc = _dot(tri2, jnp.concatenate(_split2(g), axis=0))
        run_side_job()

        w_levels = []
        seg = c
        while seg > 2:
            second = (row_in_chunk & (seg - 1)) >= (seg // 2)
            w_levels.append((seg, _level_operand(q, k, bc, seg, second).astype(BF16)))
            seg //= 2
        odd = (row_in_chunk & 1) == 1
        w_levels.append((2, jnp.where(odd, q * f, k).astype(BF16)))

        qd = (q * jnp.exp2(bc)).astype(BF16)
        b_last = bc[c - 1:c, :]
        kd = (k * jnp.exp2(b_last - bc)).astype(BF16)
        d_last = jnp.exp2(b_last)
        qk = q * k
        run_side_job()

        for hp in range(HEADS // 2):
            pair = slice(2 * hp * HEAD_DIM, 2 * (hp + 1) * HEAD_DIM)
            scores2 = jnp.zeros((c, 2 * c), F32)
            for seg, w in w_levels:
                w2 = w[:, pair]
                scores2 = jnp.where(lv2 == seg, _dot_nt(w2, _block_diag(w2[:, :HEAD_DIM], w2[:, HEAD_DIM:])), scores2)
            st2 = st_ref[:, pair]
            vt = [v[:, pair][:, i * HEAD_DIM:(i + 1) * HEAD_DIM].T.astype(BF16) for i in range(2)]
            kd2 = kd[:, pair]
            upd = _dot(jnp.concatenate(vt, axis=1), _block_diag(kd2[:, :HEAD_DIM], kd2[:, HEAD_DIM:]))
            st_ref[:, pair] = st2 * d_last[:, pair] + upd
            for i in range(2):
                sl = slice((2 * hp + i) * HEAD_DIM, (2 * hp + i + 1) * HEAD_DIM)
                lhs = jnp.concatenate([scores2[:, i * c:(i + 1) * c].astype(BF16), qd[:, sl]], axis=1)
                rhs_t = jnp.concatenate([vt[i], st2[:, i * HEAD_DIM:(i + 1) * HEAD_DIM].astype(BF16)], axis=1)
                o = _dot_nt(lhs, rhs_t) + jnp.sum(qk[:, sl], axis=-1, keepdims=True) * v[:, sl]
                o = o * lax.rsqrt(jnp.mean(o * o, axis=-1, keepdims=True) + RMS_EPS) * ng[:, sl]
                o_ref[rows, sl] = (o * gate[:, sl]).astype(BF16)
                run_side_job()

    while side_jobs:
        run_side_job()
    y = _dot(o_ref[...], w_out_ref[...])
    return _residual_layer_norm(x_ref[...], y, lng_ref[...], lnb_ref[...])


def _hgrn_layer(x2d, lb_logits, w_in, norm_g, w_out, ln_g, ln_b, *, layer, in_tb, out_tb):
    nt = SEQ // HGRN_TILE
    ns = nt // 2
    pair = (2 * HGRN_TILE, D_MODEL)
    tile = (HGRN_TILE, D_MODEL)
    tb_spec = pl.BlockSpec(pair, lambda b, c: (c, b))
    bt_spec = pl.BlockSpec(pair, lambda b, c: (b * ns + c, 0))

    def next_tile(b, c):
        return jnp.minimum(2 * (b * ns + c) + 2, BATCH * nt - 1)

    tb_next = pl.BlockSpec(tile, lambda b, c: (next_tile(b, c) % nt, next_tile(b, c) // nt))
    bt_next = pl.BlockSpec(tile, lambda b, c: (next_tile(b, c), 0))
    out_shape = (SEQ, BATCH * D_MODEL) if out_tb else (BATCH * SEQ, D_MODEL)
    lv = _hgrn_level_table(HGRN_CHUNK)
    lv2 = jnp.asarray(np.concatenate([lv, lv], axis=1))
    tri = np.tril(np.ones((HGRN_CHUNK, HGRN_CHUNK), np.float32))
    tri2 = jnp.asarray(np.concatenate([tri, tri], axis=1), dtype=BF16)
    return pl.pallas_call(
        functools.partial(_hgrn_kernel, layer),
        grid=(BATCH, ns),
        in_specs=[
            tb_spec if in_tb else bt_spec,
            tb_next if in_tb else bt_next,
            _const_spec((DEPTH, D_MODEL)),
            _const_spec((HGRN_CHUNK, 2 * HGRN_CHUNK)),
            _const_spec((HGRN_CHUNK, 2 * HGRN_CHUNK)),
            _const_spec((D_MODEL, 4 * D_MODEL)),
            _const_spec((1, D_MODEL)),
            _const_spec((D_MODEL, D_MODEL)),
            _const_spec((1, D_MODEL)),
            _const_spec((1, D_MODEL)),
        ],
        out_specs=tb_spec if out_tb else bt_spec,
        out_shape=jax.ShapeDtypeStruct(out_shape, F32),
        scratch_shapes=[
            pltpu.VMEM((HEAD_DIM, D_MODEL), F32),
            pltpu.VMEM((2, HGRN_TILE, D_MODEL), BF16),
            pltpu.VMEM((2, HGRN_TILE, D_MODEL), BF16),
            pltpu.VMEM((HGRN_TILE, 4 * D_MODEL), F32),
            pltpu.VMEM((HGRN_TILE, 4 * D_MODEL), F32),
        ],
        compiler_params=pltpu.CompilerParams(
            dimension_semantics=("arbitrary", "arbitrary"),
            vmem_limit_bytes=VMEM_LIMIT_BYTES),
        name=f"hgrn2_layer{layer}",
    )(x2d, x2d, lb_logits, lv2, tri2, w_in.astype(BF16), norm_g.reshape(1, D_MODEL), w_out.astype(BF16),
      ln_g.reshape(1, D_MODEL), ln_b.reshape(1, D_MODEL))


def _moba_proj_kernel(x_ref, w_in_ref, cq_ref, sq_ref, ck_ref, sk_ref, q_ref, k_ref, vt_ref, gate_ref):
    z = _dot(x_ref[...].astype(BF16), w_in_ref[...])
    cq, sq, ck, sk = cq_ref[...], sq_ref[...], ck_ref[...], sk_ref[...]
    for h in range(HEADS):
        qh = z[:, h * HEAD_DIM:(h + 1) * HEAD_DIM]
        kh = z[:, D_MODEL + h * HEAD_DIM:D_MODEL + (h + 1) * HEAD_DIM]
        vh = z[:, 2 * D_MODEL + h * HEAD_DIM:2 * D_MODEL + (h + 1) * HEAD_DIM]
        q_ref[0, h] = (qh * cq + pltpu.roll(qh, HEAD_DIM // 2, 1) * sq).astype(BF16)
        k_ref[0, h] = (kh * ck + pltpu.roll(kh, HEAD_DIM // 2, 1) * sk).astype(BF16)
        vt_ref[0, h] = vh.T.astype(BF16)
    gate_ref[...] = jax.nn.silu(z[:, 3 * D_MODEL:4 * D_MODEL]).astype(BF16)


def _moba_attn_kernel(q_ref, k_ref, vt_ref, gate_ref, x_ref, w_out_ref, lng_ref, lnb_ref,
                      out_ref, km_ref, o_ref, s_ref):
    n = pl.program_id(1)
    blk = MOBA_BLOCK
    nb = SEQ // blk

    @pl.when(n == 0)
    def _():
        for h in range(HEADS):
            kf = k_ref[0, h].astype(F32).reshape(nb, blk, HEAD_DIM)
            km_ref[h] = jnp.mean(kf, axis=1)

    key = lax.broadcasted_iota(jnp.int32, (blk, blk), 0)
    qry = lax.broadcasted_iota(jnp.int32, (blk, blk), 1)
    causal = key <= qry

    def attend(n_past):
        blocks = [slice(j * blk, (j + 1) * blk) for j in range(n_past + 1)]

        def selection(h, q):
            if n_past <= MOBA_TOPK:
                return None
            km = km_ref[h]
            km_hi = km.astype(BF16)
            km_lo = (km - km_hi.astype(F32)).astype(BF16)
            gsc = _dot_nt(km_hi, q) + _dot_nt(km_lo, q)
            g = [gsc[j:j + 1, :] for j in range(n_past)]
            sel = []
            for j in range(n_past):
                cnt = jnp.zeros((1, blk), F32)
                for i in range(n_past):
                    if i != j:
                        beats = (g[i] >= g[j]) if i < j else (g[i] > g[j])
                        cnt = cnt + beats.astype(F32)
                sel.append(cnt < MOBA_TOPK)
            return sel

        def score_block(h, q, sel, j, slot):
            s = _dot_nt(k_ref[0, h, blocks[j], :], q)
            if j == n_past:
                s = jnp.where(causal, s, NEG)
            elif sel is not None:
                s = jnp.where(sel[j], s, NEG)
            s_ref[slot, blocks[j], :] = s
            return jnp.max(s, axis=0, keepdims=True)

        def value_block(h, m, j, slot):
            p = jnp.exp2(s_ref[slot, blocks[j], :] - m)
            return _dot(vt_ref[0, h, :, blocks[j]], p.astype(BF16)), jnp.sum(p, axis=0, keepdims=True)

        def stage(h_cur, m_cur, h_next):
            if h_next is not None:
                q = q_ref[0, h_next]
                sel = selection(h_next, q)
            m_next = acc = l = None
            for j in range(n_past + 1):
                if h_next is not None:
                    mj = score_block(h_next, q, sel, j, h_next % 2)
                    m_next = mj if m_next is None else jnp.maximum(m_next, mj)
                if h_cur is not None:
                    aj, lj = value_block(h_cur, m_cur, j, h_cur % 2)
                    acc = aj if acc is None else acc + aj
                    l = lj if l is None else l + lj
            if h_cur is not None:
                o_ref[h_cur] = (acc * (1.0 / l)).T
            return m_next

        m = stage(None, None, 0)
        for h in range(HEADS):
            m = stage(h, m, h + 1 if h + 1 < HEADS else None)

    for n_past in range(nb):
        pl.when(n == n_past)(functools.partial(attend, n_past))

    o = jnp.concatenate([o_ref[h] for h in range(HEADS)], axis=1)
    o = (o * gate_ref[...].astype(F32)).astype(BF16)
    y = _dot(o, w_out_ref[...])
    out_ref[...] = _residual_layer_norm(x_ref[...], y, lng_ref[...], lnb_ref[...])


def _rope_tables():
    pos = np.arange(SEQ, dtype=np.float32)
    inv_freq = (1.0 / (np.float32(ROPE_THETA) ** (np.arange(0, HEAD_DIM, 2, dtype=np.float32) / HEAD_DIM))
                ).astype(np.float32)
    ang = pos[:, None] * inv_freq[None, :]
    cos, sin = np.cos(ang).astype(np.float32), np.sin(ang).astype(np.float32)
    c = np.concatenate([cos, cos], axis=1)
    s = np.concatenate([-sin, sin], axis=1)
    scale = np.float32(HEAD_DIM ** -0.5 * math.log2(math.e))
    return c * scale, s * scale, c, s


def _moba_layer(x2d, w_in, w_out, ln_g, ln_b):
    tiles_per_seq = SEQ // PROJ_TILE
    hshape = (BATCH, HEADS, SEQ, HEAD_DIM)
    hspec = pl.BlockSpec((1, HEADS, PROJ_TILE, HEAD_DIM), lambda i: (i // tiles_per_seq, 0, i % tiles_per_seq, 0))
    tspec = pl.BlockSpec((PROJ_TILE, HEAD_DIM), lambda i: (i % tiles_per_seq, 0))
    nb = SEQ // MOBA_BLOCK
    vt_shape = (BATCH, HEADS, HEAD_DIM, SEQ)
    vt_spec = pl.BlockSpec((1, HEADS, HEAD_DIM, PROJ_TILE),
                           lambda i: (i // tiles_per_seq, 0, 0, i % tiles_per_seq))
    cq, sq, ck, sk = (jnp.asarray(t) for t in _rope_tables())
    q, k, vt, gate = pl.pallas_call(
        _moba_proj_kernel,
        grid=(BATCH * tiles_per_seq,),
        in_specs=[
            pl.BlockSpec((PROJ_TILE, D_MODEL), lambda i: (i, 0)),
            _const_spec((D_MODEL, 4 * D_MODEL)),
            tspec, tspec, tspec, tspec,
        ],
        out_specs=[hspec, hspec, vt_spec, pl.BlockSpec((PROJ_TILE, D_MODEL), lambda i: (i, 0))],
        out_shape=[jax.ShapeDtypeStruct(hshape, BF16)] * 2 + [jax.ShapeDtypeStruct(vt_shape, BF16),
                                                              jax.ShapeDtypeStruct((BATCH * SEQ, D_MODEL), BF16)],
        compiler_params=pltpu.CompilerParams(
            dimension_semantics=("arbitrary",), vmem_limit_bytes=VMEM_LIMIT_BYTES,
            allow_input_fusion=[False, True, False, False, False, False]),
        name="moba_proj",
    )(x2d, w_in.astype(BF16), cq, sq, ck, sk)

    k_spec = pl.BlockSpec((1, HEADS, SEQ, HEAD_DIM), lambda b, n: (b, 0, 0, 0))
    vt_in_spec = pl.BlockSpec((1, HEADS, HEAD_DIM, SEQ), lambda b, n: (b, 0, 0, 0))
    row_spec = pl.BlockSpec((MOBA_BLOCK, D_MODEL), lambda b, n: (b * nb + n, 0))
    return pl.pallas_call(
        _moba_attn_kernel,
        grid=(BATCH, nb),
        in_specs=[
            pl.BlockSpec((1, HEADS, MOBA_BLOCK, HEAD_DIM), lambda b, n: (b, 0, n, 0)),
            k_spec, vt_in_spec, row_spec, row_spec,
            _const_spec((D_MODEL, D_MODEL)),
            _const_spec((1, D_MODEL)),
            _const_spec((1, D_MODEL)),
        ],
        out_specs=pl.BlockSpec((MOBA_BLOCK, D_MODEL), lambda b, n: (n, b)),
        out_shape=jax.ShapeDtypeStruct((SEQ, BATCH * D_MODEL), F32),
        scratch_shapes=[
            pltpu.VMEM((HEADS, nb, HEAD_DIM), F32),
            pltpu.VMEM((HEADS, MOBA_BLOCK, HEAD_DIM), F32),
            pltpu.VMEM((2, SEQ, MOBA_BLOCK), F32),
        ],
        compiler_params=pltpu.CompilerParams(
            dimension_semantics=("arbitrary", "arbitrary"), vmem_limit_bytes=VMEM_LIMIT_BYTES),
        name="moba_attn",
    )(q, k, vt, gate, x2d, w_out.astype(BF16), ln_g.reshape(1, D_MODEL), ln_b.reshape(1, D_MODEL))


def _s5_discretize_kernel(ar_ref, ai_ref, ldt_ref, br_ref, bi_ref, abr_ref, abi_ref, bbr_ref, bbi_ref):
    ar, ai = ar_ref[...], ai_ref[...]
    dt = jnp.exp(ldt_ref[...])
    mag = jnp.exp(dt * ar)
    abr = mag * jnp.cos(dt * ai)
    abi = mag * jnp.sin(dt * ai)
    nr, ni = abr - 1.0, abi
    den = ar * ar + ai * ai
    zr = (nr * ar + ni * ai) / den
    zi = (ni * ar - nr * ai) / den
    br, bi = br_ref[...], bi_ref[...]
    abr_ref[...] = abr
    abi_ref[...] = abi
    bbr_ref[...] = zr * br - zi * bi
    bbi_ref[...] = zr * bi + zi * br


def _s5_kernel(x_ref, w_in_ref, bd_ref, lam_ref, cd_ref, d_ref, w_glu_ref, b_glu_ref, w_out_ref,
               lng_ref, lnb_ref, out_ref, hst_ref, xs_ref, hb_ref, y_ref, xt_ref, ot_ref):
    ns = S5_SLAB_STATES
    lane_slabs = D_MODEL // LANES

    @pl.when(pl.program_id(0) == 0)
    def _():
        hst_ref[...] = jnp.zeros_like(hst_ref)

    half_steps = S5_STEPS // 2
    half_rows = half_steps * BATCH
    x_halves, z_halves = [], []
    for th in range(2):
        for b in range(BATCH):
            for ls in range(lane_slabs):
                col = b * D_MODEL + ls * LANES
                xt_ref[ls, pl.ds(th * half_rows + b, half_steps, stride=BATCH), :] = \
                    x_ref[th * half_steps:(th + 1) * half_steps, col:col + LANES]
        xh = jnp.concatenate([xt_ref[ls, th * half_rows:(th + 1) * half_rows, :] for ls in range(lane_slabs)],
                             axis=1)
        x_halves.append(xh)
        z_halves.append(_dot(xh.astype(BF16), w_in_ref[...]))
    z = jnp.concatenate(z_halves, axis=0)
    u = z[:, 0:D_MODEL]
    gate = z[:, D_MODEL:2 * D_MODEL]
    u_bf = u.astype(BF16)

    for s in range(S5_NSLAB):
        xs_ref[...] = _dot(u_bf[:, s * S5_SLAB:(s + 1) * S5_SLAB], bd_ref[s])
        lr = jnp.broadcast_to(lam_ref[s, 0:1, :], (BATCH, ns))
        li = jnp.broadcast_to(lam_ref[s, 1:2, :], (BATCH, ns))

        def step(t, h):
            hr, hi = h
            r = pl.multiple_of(t * BATCH, BATCH)
            xr = xs_ref[pl.ds(r, BATCH), 0:ns]
            xi = xs_ref[pl.ds(r, BATCH), ns:2 * ns]
            nhr = lr * hr - li * hi + xr
            nhi = lr * hi + li * hr + xi
            hb_ref[pl.ds(r, BATCH), 0:ns] = nhr.astype(BF16)
            hb_ref[pl.ds(r, BATCH), ns:2 * ns] = nhi.astype(BF16)
            return nhr, nhi

        hr, hi = lax.fori_loop(0, S5_STEPS, step, (hst_ref[s, 0], hst_ref[s, 1]), unroll=True)
        hst_ref[s, 0] = hr
        hst_ref[s, 1] = hi
        y_ref[:, s * S5_SLAB:(s + 1) * S5_SLAB] = _dot(hb_ref[...], cd_ref[s])

    y = y_ref[...] + d_ref[...] * u
    y = jax.nn.gelu(y)
    y = y * jax.nn.sigmoid(_dot(y.astype(BF16), w_glu_ref[...]) + b_glu_ref[...])
    y = y * jax.nn.silu(gate)
    y_bf = y.astype(BF16)
    for th in range(2):
        hrows = slice(th * half_rows, (th + 1) * half_rows)
        out = _dot(y_bf[hrows], w_out_ref[...])
        res = _residual_layer_norm(x_halves[th], out, lng_ref[...], lnb_ref[...])
        for ls in range(lane_slabs):
            ot_ref[ls, hrows, :] = res[:, ls * LANES:(ls + 1) * LANES]
        for b in range(BATCH):
            for ls in range(lane_slabs):
                col = b * D_MODEL + ls * LANES
                out_ref[th * half_steps:(th + 1) * half_steps, col:col + LANES] = \
                    ot_ref[ls, pl.ds(th * half_rows + b, half_steps, stride=BATCH), :]


def _s5_layer(x_sbd, w_in, a_re, a_im, log_dt, b_re, b_im, c_re, c_im, d_skip, w_glu, b_glu, w_out, ln_g, ln_b):
    gh = S5_GROUPS * S5_GROUP
    rep = lambda a: jnp.repeat(a, S5_GROUP, axis=0)
    small = jax.ShapeDtypeStruct((gh, S5_STATE), F32)
    abr, abi, bbr, bbi = pl.pallas_call(
        _s5_discretize_kernel,
        out_shape=[small] * 4,
        name="s5_discretize",
    )(rep(a_re), rep(a_im), rep(jnp.broadcast_to(log_dt[:, None], (S5_GROUPS, S5_STATE))),
      b_re.transpose(0, 2, 1).reshape(gh, S5_STATE), b_im.transpose(0, 2, 1).reshape(gh, S5_STATE))

    gl = S5_SLAB // S5_GROUP
    eye = jnp.eye(gl, dtype=F32)

    def block_diag_in(bb):
        t = bb.reshape(S5_NSLAB, gl, S5_GROUP, S5_STATE)
        return jnp.einsum('sghp,gk->sghkp', t, eye).reshape(S5_NSLAB, S5_SLAB, gl * S5_STATE)

    def block_diag_out(cc):
        t = cc.reshape(S5_NSLAB, gl, S5_GROUP, S5_STATE)
        return jnp.einsum('sghp,gk->skpgh', t, eye).reshape(S5_NSLAB, gl * S5_STATE, S5_SLAB)

    bd = jnp.concatenate([block_diag_in(bbr), block_diag_in(bbi)], axis=2).astype(BF16)
    cd = jnp.concatenate([block_diag_out(c_re), -block_diag_out(c_im)], axis=1).astype(BF16)
    lam = jnp.stack([abr[::S5_GROUP].reshape(S5_NSLAB, gl * S5_STATE),
                     abi[::S5_GROUP].reshape(S5_NSLAB, gl * S5_STATE)], axis=1)

    rows = S5_STEPS * BATCH
    ns2 = 2 * S5_SLAB_STATES
    row_spec = pl.BlockSpec((S5_STEPS, BATCH * D_MODEL), lambda i: (i, 0))
    return pl.pallas_call(
        _s5_kernel,
        grid=(SEQ // S5_STEPS,),
        in_specs=[
            row_spec,
            _const_spec((D_MODEL, 2 * D_MODEL)),
            _const_spec((S5_NSLAB, S5_SLAB, ns2)),
            _const_spec((S5_NSLAB, 2, S5_SLAB_STATES)),
            _const_spec((S5_NSLAB, ns2, S5_SLAB)),
            _const_spec((1, D_MODEL)),
            _const_spec((D_MODEL, D_MODEL)),
            _const_spec((1, D_MODEL)),
            _const_spec((D_MODEL, D_MODEL)),
            _const_spec((1, D_MODEL)),
            _const_spec((1, D_MODEL)),
        ],
        out_specs=row_spec,
        out_shape=jax.ShapeDtypeStruct((SEQ, BATCH * D_MODEL), F32),
        scratch_shapes=[
            pltpu.VMEM((S5_NSLAB, 2, BATCH, S5_SLAB_STATES), F32),
            pltpu.VMEM((rows, ns2), F32),
            pltpu.VMEM((rows, ns2), BF16),
            pltpu.VMEM((rows, D_MODEL), F32),
            pltpu.VMEM((D_MODEL // LANES, rows, LANES), F32),
            pltpu.VMEM((D_MODEL // LANES, rows, LANES), F32),
        ],
        compiler_params=pltpu.CompilerParams(
            dimension_semantics=("arbitrary",), vmem_limit_bytes=VMEM_LIMIT_BYTES),
        name="s5_layer",
    )(x_sbd, w_in.astype(BF16), bd, lam, cd, d_skip.reshape(1, D_MODEL), w_glu.astype(BF16),
      b_glu.reshape(1, D_MODEL), w_out.astype(BF16), ln_g.reshape(1, D_MODEL), ln_b.reshape(1, D_MODEL))


def kernel(x, hgrn_lower_bounds,
           l0_w_in, l0_norm_g, l0_w_out, l0_ln_g, l0_ln_b,
           l1_w_in, l1_w_out, l1_ln_g, l1_ln_b,
           l2_w_in, l2_a_re, l2_a_im, l2_log_dt, l2_b_re, l2_b_im, l2_c_re, l2_c_im,
           l2_d, l2_w_glu, l2_b_glu, l2_w_out, l2_ln_g, l2_ln_b,
           l3_w_in, l3_norm_g, l3_w_out, l3_ln_g, l3_ln_b):
    h = x.reshape(BATCH * SEQ, D_MODEL)
    h = _hgrn_layer(h, hgrn_lower_bounds, l0_w_in, l0_norm_g, l0_w_out, l0_ln_g, l0_ln_b,
                    layer=0, in_tb=False, out_tb=False)
    h = _moba_layer(h, l1_w_in, l1_w_out, l1_ln_g, l1_ln_b)
    h = _s5_layer(h, l2_w_in, l2_a_re, l2_a_im, l2_log_dt, l2_b_re, l2_b_im,
                  l2_c_re, l2_c_im, l2_d, l2_w_glu, l2_b_glu, l2_w_out, l2_ln_g, l2_ln_b)
    h = _hgrn_layer(h, hgrn_lower_bounds, l3_w_in, l3_norm_g, l3_w_out,
                    l3_ln_g, l3_ln_b, layer=3, in_tb=True, out_tb=False)
    return h.reshape(BATCH, SEQ, D_MODEL)
```
